```python
import math
import jax, jax.numpy as jnp
from jax import lax
import numpy as np

D_MODEL = 1024
BATCH = 8
SEQ = 4096
DEPTH = 4

MEM_LEN = 256
SSM_WIDTH = D_MODEL // 2
SSM_GROUP = 16
SSM_GROUPS = SSM_WIDTH // SSM_GROUP
SSM_STATE = 64
DT_MIN = 1e-3
DT_MAX = 1e-1
MLA_HEADS = 8
MLA_NOPE = 64
MLA_ROPE = 32
MLA_V = 64
MLA_Q_RANK = 256
MLA_KV_RANK = 128
MLA_WIDTH = MLA_HEADS * MLA_V
ROPE_THETA = 10000.0
Q_BLOCK = 128
X_HEADS = 4
X_HEAD_DIM = 128
X_WIDTH = X_HEADS * X_HEAD_DIM
N_BRANCH = 3
IN_WIDTHS = (SSM_WIDTH, SSM_WIDTH, MLA_Q_RANK, MLA_KV_RANK, MLA_ROPE, MLA_WIDTH, X_WIDTH, X_WIDTH, N_BRANCH * D_MODEL)
D_IN = sum(IN_WIDTHS)
ALPHA = (2 * DEPTH) ** 0.25
BETA = (8 * DEPTH) ** -0.25
NORM_EPS = 1e-5
POS_OFFSET_MAX = 1024

kernel_name = 'hybrid_s5_mla_memory_gated_deepnorm'


def _layer_norm(x, g, b):
    xf = x.astype(jnp.float32)
    mu = jnp.mean(xf, axis=-1, keepdims=True)
    var = jnp.mean(jnp.square(xf - mu), axis=-1, keepdims=True)
    y = (xf - mu) * lax.rsqrt(var + NORM_EPS) * g.astype(jnp.float32) + b.astype(jnp.float32)
    return y.astype(x.dtype)


def _rms_norm(x, g):
    xf = x.astype(jnp.float32)
    y = xf * lax.rsqrt(jnp.mean(jnp.square(xf), axis=-1, keepdims=True) + NORM_EPS) * g.astype(jnp.float32)
    return y.astype(x.dtype)


def _rope_tables(positions):
    inv_freq = ROPE_THETA ** (-jnp.arange(0, MLA_ROPE, 2, dtype=jnp.float32) / MLA_ROPE)
    ang = positions.astype(jnp.float32)[..., None] * inv_freq
    return jnp.cos(ang)[:, :, None, :], jnp.sin(ang)[:, :, None, :]


def _apply_rope(t, cos, sin):
    tf = t.astype(jnp.float32)
    t1, t2 = jnp.split(tf, 2, axis=-1)
    out = jnp.concatenate([t1 * cos - t2 * sin, t1 * sin + t2 * cos], axis=-1)
    return out.astype(t.dtype)


def _complex_scan_combine(e1, e2):
    a1r, a1i, b1r, b1i = e1
    a2r, a2i, b2r, b2i = e2
    ar = a1r * a2r - a1i * a2i
    ai = a1r * a2i + a1i * a2r
    br = a2r * b1r - a2i * b1i + b2r
    bi = a2r * b1i + a2i * b1r + b2i
    return ar, ai, br, bi


def _s5_ssm(u, a_re, a_im, log_dt, b_re, b_im, c_re, c_im, d_skip):
    bsz, s, _ = u.shape
    f32 = jnp.float32
    uf = u.astype(f32).reshape(bsz, s, SSM_GROUPS, SSM_GROUP)
    dt = jnp.exp(log_dt.astype(f32))[:, None]
    lr, li = a_re.astype(f32), a_im.astype(f32)
    mag = jnp.exp(lr * dt)
    lb_re = mag * jnp.cos(li * dt)
    lb_im = mag * jnp.sin(li * dt)
    nr, ni = lb_re - 1.0, lb_im
    den = lr * lr + li * li
    f_re = (nr * lr + ni * li) / den
    f_im = (ni * lr - nr * li) / den
    br, bi = b_re.astype(f32), b_im.astype(f32)
    bb_re = f_re[..., None] * br - f_im[..., None] * bi
    bb_im = f_re[..., None] * bi + f_im[..., None] * br
    bu_re = jnp.einsum('bsgc,gpc->bsgp', uf, bb_re)
    bu_im = jnp.einsum('bsgc,gpc->bsgp', uf, bb_im)
    a_re_t = jnp.broadcast_to(lb_re, bu_re.shape)
    a_im_t = jnp.broadcast_to(lb_im, bu_im.shape)
    _, _, h_re, h_im = lax.associative_scan(_complex_scan_combine, (a_re_t, a_im_t, bu_re, bu_im), axis=1)
    y = (jnp.einsum('bsgp,gcp->bsgc', h_re, c_re.astype(f32))
         - jnp.einsum('bsgp,gcp->bsgc', h_im, c_im.astype(f32)))
    y = y.reshape(bsz, s, SSM_WIDTH) + d_skip.astype(f32) * u.astype(f32)
    return y.astype(u.dtype)


def _s5_glu(y, w_glu, b_glu):
    g = jax.nn.gelu(y)
    a, b = jnp.split(g @ w_glu + b_glu, 2, axis=-1)
    return a * jax.nn.sigmoid(b)


def _mla_attention(c_q, c_kv, k_rope_in, q_norm, w_uq, kv_norm, w_ukv, cos, sin):
    bsz, s, _ = c_q.shape
    q = (_rms_norm(c_q, q_norm) @ w_uq).reshape(bsz, s, MLA_HEADS, MLA_NOPE + MLA_ROPE)
    q_nope = q[..., :MLA_NOPE]
    q_rope = _apply_rope(q[..., MLA_NOPE:], cos, sin)
    kv = (_rms_norm(c_kv, kv_norm) @ w_ukv).reshape(bsz, s, MLA_HEADS, MLA_NOPE + MLA_V)
    k_nope, v = kv[..., :MLA_NOPE], kv[..., MLA_NOPE:]
    k_rope = _apply_rope(k_rope_in[:, :, None, :], cos, sin)[:, :, 0, :]
    scale = (MLA_NOPE + MLA_ROPE) ** -0.5
    neg = jnp.finfo(jnp.float32).min
    outs = []
    for blk in range(s // Q_BLOCK):
        q0 = blk * Q_BLOCK
        kend = q0 + Q_BLOCK
        sc = (jnp.einsum('bqhd,bkhd->bhqk', q_nope[:, q0:kend], k_nope[:, :kend])
              + jnp.einsum('bqhr,bkr->bhqk', q_rope[:, q0:kend], k_rope[:, :kend]))
        sc = sc.astype(jnp.float32) * scale
        causal = (q0 + jnp.arange(Q_BLOCK))[:, None] >= jnp.arange(kend)[None, :]
        p = jax.nn.softmax(jnp.where(causal, sc, neg), axis=-1)
        outs.append(jnp.einsum('bhqk,bkhd->bqhd', p.astype(v.dtype), v[:, :kend]))
    return jnp.concatenate(outs, axis=1).reshape(bsz, s, MLA_WIDTH)


def _memory_attention(q_in, mem, w_mem_kv):
    bsz, s, _ = q_in.shape
    m = mem.shape[1]
    kv = (mem @ w_mem_kv).reshape(bsz, m, 2, X_HEADS, X_HEAD_DIM)
    k, v = kv[:, :, 0], kv[:, :, 1]
    q = q_in.reshape(bsz, s, X_HEADS, X_HEAD_DIM)
    sc = jnp.einsum('bshd,bmhd->bhsm', q, k).astype(jnp.float32) * X_HEAD_DIM ** -0.5
    p = jax.nn.softmax(sc, axis=-1)
    return jnp.einsum('bhsm,bmhd->bshd', p.astype(v.dtype), v).reshape(bsz, s, X_WIDTH)


def setup_inputs(seed: int = 0) -> dict:
    key = jax.random.key(seed)
    ks = jax.random.split(key, 26)
    f32 = jnp.float32

    def nrm(k, shape, scale):
        return scale * jax.random.normal(k, shape, f32)

    x = nrm(ks[0], (BATCH, SEQ, D_MODEL), 1.0)
    mem = nrm(ks[1], (BATCH, MEM_LEN, D_MODEL), 1.0)
    offsets = jax.random.randint(ks[2], (BATCH, 1), 0, POS_OFFSET_MAX, dtype=jnp.int32)
    positions = offsets + jnp.arange(SEQ, dtype=jnp.int32)[None, :]
    w_in = nrm(ks[3], (DEPTH, D_MODEL, D_IN), D_MODEL ** -0.5)
    b_gate = nrm(ks[4], (DEPTH, N_BRANCH * D_MODEL), 0.01)
    ssm_a_re = -0.5 + nrm(ks[5], (DEPTH, SSM_GROUPS, SSM_STATE), 0.01)
    ssm_a_im = math.pi * jnp.arange(SSM_STATE, dtype=f32) + nrm(ks[6], (DEPTH, SSM_GROUPS, SSM_STATE), 0.01)
    ssm_log_dt = jax.random.uniform(ks[7], (DEPTH, SSM_GROUPS), f32, math.log(DT_MIN), math.log(DT_MAX))
    ssm_b_re = nrm(ks[8], (DEPTH, SSM_GROUPS, SSM_STATE, SSM_GROUP), (2 * SSM_GROUP) ** -0.5)
    ssm_b_im = nrm(ks[9], (DEPTH, SSM_GROUPS, SSM_STATE, SSM_GROUP), (2 * SSM_GROUP) ** -0.5)
    ssm_c_re = nrm(ks[10], (DEPTH, SSM_GROUPS, SSM_GROUP, SSM_STATE), (2 * SSM_STATE) ** -0.5)
    ssm_c_im = nrm(ks[11], (DEPTH, SSM_GROUPS, SSM_GROUP, SSM_STATE), (2 * SSM_STATE) ** -0.5)
    ssm_d = nrm(ks[12], (DEPTH, SSM_WIDTH), 1.0)
    w_glu = nrm(ks[13], (DEPTH, SSM_WIDTH, 2 * SSM_WIDTH), SSM_WIDTH ** -0.5)
    b_glu = nrm(ks[14], (DEPTH, 2 * SSM_WIDTH), 0.01)
    mla_q_norm = 1.0 + nrm(ks[15], (DEPTH, MLA_Q_RANK), 0.01)
    w_uq = nrm(ks[16], (DEPTH, MLA_Q_RANK, MLA_HEADS * (MLA_NOPE + MLA_ROPE)), MLA_Q_RANK ** -0.5)
    mla_kv_norm = 1.0 + nrm(ks[17], (DEPTH, MLA_KV_RANK), 0.01)
    w_ukv = nrm(ks[18], (DEPTH, MLA_KV_RANK, MLA_HEADS * (MLA_NOPE + MLA_V)), MLA_KV_RANK ** -0.5)
    w_mem_kv = nrm(ks[19], (DEPTH, D_MODEL, 2 * X_WIDTH), D_MODEL ** -0.5)
    p_ssm = nrm(ks[20], (DEPTH, SSM_WIDTH, D_MODEL), BETA * SSM_WIDTH ** -0.5)
    p_mla = nrm(ks[21], (DEPTH, MLA_WIDTH, D_MODEL), BETA * MLA_WIDTH ** -0.5)
    p_mem = nrm(ks[22], (DEPTH, X_WIDTH, D_MODEL), BETA * X_WIDTH ** -0.5)
    w_out = nrm(ks[23], (DEPTH, D_MODEL, D_MODEL), BETA * D_MODEL ** -0.5)
    ln_g = 1.0 + nrm(ks[24], (DEPTH, D_MODEL), 0.01)
    ln_b = nrm(ks[25], (DEPTH, D_MODEL), 0.01)
    return {'x': x, 'mem': mem, 'positions': positions, 'w_in': w_in, 'b_gate': b_gate,
            'ssm_a_re': ssm_a_re, 'ssm_a_im': ssm_a_im, 'ssm_log_dt': ssm_log_dt,
            'ssm_b_re': ssm_b_re, 'ssm_b_im': ssm_b_im, 'ssm_c_re': ssm_c_re, 'ssm_c_im': ssm_c_im,
            'ssm_d': ssm_d, 'w_glu': w_glu, 'b_glu': b_glu,
            'mla_q_norm': mla_q_norm, 'w_uq': w_uq, 'mla_kv_norm': mla_kv_norm, 'w_ukv': w_ukv,
            'w_mem_kv': w_mem_kv, 'p_ssm': p_ssm, 'p_mla': p_mla, 'p_mem': p_mem,
            'w_out': w_out, 'ln_g': ln_g, 'ln_b': ln_b}


def reference(x, mem, positions, w_in, b_gate, ssm_a_re, ssm_a_im, ssm_log_dt, ssm_b_re, ssm_b_im,
              ssm_c_re, ssm_c_im, ssm_d, w_glu, b_glu, mla_q_norm, w_uq, mla_kv_norm, w_ukv,
              w_mem_kv, p_ssm, p_mla, p_mem, w_out, ln_g, ln_b):
    bsz, s, d = x.shape
    cos, sin = _rope_tables(positions)
    split_points = np.cumsum(IN_WIDTHS)[:-1].tolist()
    for l in range(DEPTH):
        proj = x @ w_in[l]
        u, z_ssm, c_q, c_kv, k_rope, z_mla, q_mem, z_mem, gate_logits = jnp.split(proj, split_points, axis=-1)
        gates = jax.nn.sigmoid((gate_logits + b_gate[l]).astype(jnp.float32)).astype(x.dtype)
        gates = gates.reshape(bsz, s, N_BRANCH, d)
        y_ssm = _s5_ssm(u, ssm_a_re[l], ssm_a_im[l], ssm_log_dt[l], ssm_b_re[l], ssm_b_im[l],
                        ssm_c_re[l], ssm_c_im[l], ssm_d[l])
        y_ssm = _s5_glu(y_ssm, w_glu[l], b_glu[l]) * jax.nn.silu(z_ssm)
        y_mla = _mla_attention(c_q, c_kv, k_rope, mla_q_norm[l], w_uq[l], mla_kv_norm[l], w_ukv[l], cos, sin)
        y_mla = y_mla * jax.nn.silu(z_mla)
        y_mem = _memory_attention(q_mem, mem, w_mem_kv[l]) * jax.nn.silu(z_mem)
        merged = (gates[:, :, 0] * (y_ssm @ p_ssm[l])
                  + gates[:, :, 1] * (y_mla @ p_mla[l])
                  + gates[:, :, 2] * (y_mem @ p_mem[l]))
        x = _layer_norm(ALPHA * x + merged @ w_out[l], ln_g[l], ln_b[l])
    return x
```

```python
import functools
import math

import jax
import jax.numpy as jnp
from jax import lax
from jax.experimental import pallas as pl
from jax.experimental.pallas import tpu as pltpu

F32 = jnp.float32
BF16 = jnp.bfloat16

LANES = 128
D_MODEL = 1024
SSM_WIDTH = 512
SSM_GROUP = 16
SSM_GROUPS = SSM_WIDTH // SSM_GROUP
SSM_STATE = 64
MLA_HEADS = 8
MLA_NOPE = 64
MLA_ROPE = 32
MLA_V = 64
MLA_Q_RANK = 256
MLA_KV_RANK = 128
MLA_WIDTH = MLA_HEADS * MLA_V
ROPE_THETA = 10000.0
X_HEADS = 4
X_HEAD_DIM = 128
X_WIDTH = X_HEADS * X_HEAD_DIM
N_BRANCH = 3
NORM_EPS = 1e-5

SSM_CHUNK = 32
SSM_LC = SSM_CHUNK * SSM_GROUP
SSM_GB = 4
VMEM_LIMIT = 48 * 1024 * 1024


def _sigmoid(v):
    return jax.nn.sigmoid(v)


def _silu(v):
    return v * jax.nn.sigmoid(v)


def _dot(a, b):
    return jnp.dot(a, b, preferred_element_type=F32)


def _dot_nt(a, b):
    return lax.dot_general(a, b, (((1,), (1,)), ((), ())), preferred_element_type=F32)


def _params(sem):
    return pltpu.CompilerParams(dimension_semantics=sem, vmem_limit_bytes=VMEM_LIMIT)


def _rope_body(pos_ref, invf_ref, c_ref, s_ref):
    ang = pos_ref[...] * invf_ref[...]
    lane = lax.broadcasted_iota(jnp.int32, ang.shape, 1)
    rope = (lane >= MLA_NOPE) & (lane < MLA_NOPE + MLA_ROPE)
    c_ref[...] = jnp.where(lane < MLA_NOPE, 1.0, jnp.where(rope, jnp.cos(ang), 0.0))
    s_ref[...] = jnp.where(rope, jnp.sin(ang), 0.0)


def _rope_tables(pos_b, invf, tm):
    t = pos_b.shape[0]
    spec = pl.BlockSpec((tm, LANES), lambda i: (i, 0))
    return pl.pallas_call(
        _rope_body,
        grid=(t // tm,),
        in_specs=[spec, pl.BlockSpec((1, LANES), lambda i: (0, 0))],
        out_specs=[spec, spec],
        out_shape=[jax.ShapeDtypeStruct((t, LANES), F32)] * 2,
        compiler_params=_params(("parallel",)),
        name="rope_tables",
    )(pos_b, invf)


def _memkv_body(mem_ref, w_ref, o_ref):
    o_ref[0] = _dot(mem_ref[...], w_ref[0]).astype(BF16)


def _memkv(mem_b, w_b):
    depth = w_b.shape[0]
    rows = mem_b.shape[0]
    return pl.pallas_call(
        _memkv_body,
        grid=(depth,),
        in_specs=[pl.BlockSpec((rows, D_MODEL), lambda l: (0, 0)),
                  pl.BlockSpec((1, D_MODEL, 2 * X_WIDTH), lambda l: (l, 0, 0))],
        out_specs=pl.BlockSpec((1, rows, 2 * X_WIDTH), lambda l: (l, 0, 0)),
        out_shape=jax.ShapeDtypeStruct((depth, rows, 2 * X_WIDTH), BF16),
        compiler_params=_params(("parallel",)),
        name="mem_kv",
    )(mem_b, w_b)


_C_U = 0
_C_ZS = _C_U + SSM_WIDTH
_C_CQ = _C_ZS + SSM_WIDTH
_C_CKV = _C_CQ + MLA_Q_RANK
_C_KR = _C_CKV + MLA_KV_RANK
_C_ZM = _C_KR + 2 * LANES
_C_QX = _C_ZM + MLA_WIDTH
_C_ZX = _C_QX + X_WIDTH
_C_END = _C_ZX + X_WIDTH
_QW = MLA_HEADS * LANES


def _rms(v, g):
    return v * lax.rsqrt(jnp.mean(jnp.square(v), axis=-1, keepdims=True) + NORM_EPS) * g


def _inproj_body(x_ref, w1_ref, wq_ref, wkv_ref, qn_ref, kvn_ref, c_ref, s_ref,
                 u_ref, szs_ref, q_ref, k_ref, v_ref, szm_ref, qx_ref, szx_ref):
    xb = x_ref[...].astype(BF16)

    def proj(c0, c1):
        return _dot(xb, w1_ref[0, :, c0:c1])

    u_ref[...] = proj(_C_U, _C_ZS).astype(BF16)
    szs_ref[...] = _silu(proj(_C_ZS, _C_CQ)).astype(BF16)
    szm_ref[...] = _silu(proj(_C_ZM, _C_QX)).astype(BF16)
    qx_ref[...] = proj(_C_QX, _C_ZX).astype(BF16)
    szx_ref[...] = _silu(proj(_C_ZX, _C_END)).astype(BF16)

    cos = c_ref[...]
    sin = s_ref[...]
    scale = (MLA_NOPE + MLA_ROPE) ** -0.5
    cos_q = cos * scale
    sin_q = sin * scale

    cq = _rms(proj(_C_CQ, _C_CKV), qn_ref[0]).astype(BF16)
    qa = _dot(cq, wq_ref[0])
    for h in range(MLA_HEADS):
        a = qa[:, h * LANES:(h + 1) * LANES]
        b = qa[:, _QW + h * LANES:_QW + (h + 1) * LANES]
        q_ref[:, h * LANES:(h + 1) * LANES] = (a * cos_q + b * sin_q).astype(BF16)

    ckv = _rms(proj(_C_CKV, _C_KR), kvn_ref[0]).astype(BF16)
    kv = _dot(ckv, wkv_ref[0])
    kr = proj(_C_KR, _C_ZM)
    kr = kr[:, :LANES] * cos + kr[:, LANES:] * sin
    for h in range(MLA_HEADS):
        k_ref[:, h * LANES:(h + 1) * LANES] = (kv[:, h * LANES:(h + 1) * LANES] + kr).astype(BF16)
    v_ref[...] = kv[:, _QW:].astype(BF16)


def _inproj(x, w1, wq, wkv, qn, kvn, cos, sin, layer, tm):
    t = x.shape[0]
    row = lambda w: pl.BlockSpec((tm, w), lambda i: (i, 0))
    lsel = lambda a: pl.BlockSpec((1,) + a.shape[1:], lambda i: (layer,) + (0,) * (a.ndim - 1))
    widths = (SSM_WIDTH, SSM_WIDTH, _QW, _QW, MLA_WIDTH, MLA_WIDTH, X_WIDTH, X_WIDTH)
    return pl.pallas_call(
        _inproj_body,
        grid=(t // tm,),
        in_specs=[row(D_MODEL), lsel(w1), lsel(wq), lsel(wkv), lsel(qn), lsel(kvn), row(LANES), row(LANES)],
        out_specs=[row(w) for w in widths],
        out_shape=[jax.ShapeDtypeStruct((t, w), BF16) for w in widths],
        compiler_params=_params(("parallel",)),
        name="in_proj",
    )(x, w1, wq, wkv, qn, kvn, cos, sin)


def _ssm_prep_body(ldt_ref, a2_ref, acol_ref, bt2_ref, ce_ref,
                   toep_ref, bs_ref, cs_ref, al_ref):
    chunk = SSM_CHUNK
    dt = jnp.exp(ldt_ref[0])
    lr2 = a2_ref[0, 0:1, :]
    li2 = a2_ref[0, 1:2, :]
    lane2 = lax.broadcasted_iota(jnp.int32, (1, 2 * SSM_STATE), 1)
    first = lane2 < SSM_STATE

    mag = jnp.exp(lr2 * dt)
    lb_re = mag * jnp.cos(li2 * dt)
    lb_im = mag * jnp.sin(li2 * dt)
    nr, ni = lb_re - 1.0, lb_im
    den = lr2 * lr2 + li2 * li2
    f_re = (nr * lr2 + ni * li2) / den
    f_im = (ni * lr2 - nr * li2) / den
    bt_re = bt2_ref[0, 0]
    bt_im = bt2_ref[0, 1]
    bb_re = f_re * bt_re - f_im * bt_im
    bb_im = f_re * bt_im + f_im * bt_re

    srow = lax.broadcasted_iota(jnp.int32, (SSM_LC, 1), 0) // SSM_GROUP
    e = (chunk - 1 - srow).astype(F32)
    pm = jnp.exp(lr2 * dt * e)
    pr = pm * jnp.cos(li2 * dt * e)
    pi = pm * jnp.sin(li2 * dt * e)
    br = jnp.tile(bb_re, (chunk, 1))
    bi = jnp.tile(bb_im, (chunk, 1))
    s_re = pr * br - pi * bi
    s_im = pr * bi + pi * br
    bs_ref[0, :, :2 * SSM_STATE] = jnp.where(first, s_re, s_im).astype(BF16)
    bs_ref[0, :, 2 * SSM_STATE:] = jnp.where(first, s_im, s_re).astype(BF16)

    lr = acol_ref[0, 0]
    li = acol_ref[0, 1]
    dtc = dt[:, 0:1]
    kk = (lax.broadcasted_iota(jnp.int32, (1, SSM_LC), 1) // SSM_GROUP).astype(F32)
    c_re = ce_ref[0, 0]
    c_im = ce_ref[0, 1]

    def c_times_power(kf):
        m = jnp.exp(lr * dtc * kf)
        p_re = m * jnp.cos(li * dtc * kf)
        p_im = m * jnp.sin(li * dtc * kf)
        return c_re * p_re - c_im * p_im, c_re * p_im + c_im * p_re

    m_re, m_im = c_times_power(kk)
    bcol_re = bb_re[:, :SSM_STATE]
    bcol_im = bb_im[:, :SSM_STATE]
    hi = lax.Precision.HIGHEST
    kcat = (jnp.dot(bcol_re, m_re, precision=hi, preferred_element_type=F32)
            - jnp.dot(bcol_im, m_im, precision=hi, preferred_element_type=F32))
    lane = lax.broadcasted_iota(jnp.int32, (SSM_GROUP, SSM_LC), 1)
    for s in range(chunk):
        shifted = kcat if s == 0 else pltpu.roll(kcat, SSM_GROUP * s, 1)
        toep_ref[0, s * SSM_GROUP:(s + 1) * SSM_GROUP, :] = jnp.where(
            lane >= SSM_GROUP * s, shifted, 0.0).astype(BF16)

    n_re, n_im = c_times_power(kk + 1.0)
    cs_ref[0, :SSM_STATE, :] = n_re.astype(BF16)
    cs_ref[0, SSM_STATE:, :] = (-n_im).astype(BF16)

    lf = float(chunk)
    am = jnp.exp(lr2 * dt * lf)
    a_re = am * jnp.cos(li2 * dt * lf)
    a_im = am * jnp.sin(li2 * dt * lf)
    al_ref[0, 0:1, :] = a_re
    al_ref[0, 1:2, :] = jnp.where(first, -a_im, a_im)


def _ssm_prep(ldt, a2, acol, bt2, ce):
    n = ldt.shape[0]
    sel = lambda a: pl.BlockSpec((1,) + a.shape[1:], lambda i: (i,) + (0,) * (a.ndim - 1))
    out_shapes = [jax.ShapeDtypeStruct((n, SSM_LC, SSM_LC), BF16),
                  jax.ShapeDtypeStruct((n, SSM_LC, 4 * SSM_STATE), BF16),
                  jax.ShapeDtypeStruct((n, 2 * SSM_STATE, SSM_LC), BF16),
                  jax.ShapeDtypeStruct((n, 2, 2 * SSM_STATE), F32)]
    return pl.pallas_call(
        _ssm_prep_body,
        grid=(n,),
        in_specs=[sel(ldt), sel(a2), sel(acol), sel(bt2), sel(ce)],
        out_specs=[sel(s) for s in out_shapes],
        out_shape=out_shapes,
        compiler_params=_params(("parallel",)),
        name="ssm_prep",
    )(ldt, a2, acol, bt2, ce)


def _ssm_body(u_ref, toep_ref, bs_ref, cs_ref, al_ref, d_ref, o_ref, hloc_ref, hst_ref, *, batch, n_chunks):
    gb = u_ref.shape[0]
    for j in range(gb):
        hloc_ref[j] = _dot(u_ref[j], bs_ref[j])

    a1 = [jnp.broadcast_to(al_ref[j, 0:1, :], (batch, 2 * SSM_STATE)) for j in range(gb)]
    a2 = [jnp.broadcast_to(al_ref[j, 1:2, :], (batch, 2 * SSM_STATE)) for j in range(gb)]

    def step(c, carry):
        out = []
        r0 = pl.multiple_of(c * batch, batch)
        for j in range(gb):
            h, hs = carry[j]
            hst_ref[j, pl.ds(r0, batch), :] = h
            loc = hloc_ref[j, pl.ds(r0, batch), :]
            out.append((a1[j] * h + a2[j] * hs + loc[:, :2 * SSM_STATE],
                        a1[j] * hs - a2[j] * h + loc[:, 2 * SSM_STATE:]))
        return tuple(out)

    zero = jnp.zeros((batch, 2 * SSM_STATE), F32)
    lax.fori_loop(0, n_chunks, step, tuple((zero, zero) for _ in range(gb)))

    for j in range(gb):
        u = u_ref[j]
        y = (_dot(u, toep_ref[j]) + _dot(hst_ref[j].astype(BF16), cs_ref[j])
             + d_ref[j] * u.astype(F32))
        o_ref[j] = jax.nn.gelu(y).astype(BF16)


def _ssm(u_g, toep, bs, cs, al, d_e, layer, batch):
    groups, rows, _ = u_g.shape
    n_chunks = rows // batch
    gb = SSM_GB
    lsel = lambda a: pl.BlockSpec((gb,) + a.shape[1:],
                                  lambda i: (layer * (groups // gb) + i,) + (0,) * (a.ndim - 1))
    useg = pl.BlockSpec((gb, rows, SSM_LC), lambda i: (i, 0, 0))
    return pl.pallas_call(
        functools.partial(_ssm_body, batch=batch, n_chunks=n_chunks),
        grid=(groups // gb,),
        in_specs=[useg, lsel(toep), lsel(bs), lsel(cs), lsel(al), lsel(d_e)],
        out_specs=useg,
        out_shape=jax.ShapeDtypeStruct(u_g.shape, BF16),
        scratch_shapes=[pltpu.VMEM((gb, rows, 4 * SSM_STATE), F32),
                        pltpu.VMEM((gb, rows, 2 * SSM_STATE), F32)],
        compiler_params=_params(("parallel",)),
        name="s5_scan",
    )(u_g, toep, bs, cs, al, d_e)


def _mla_body(q_ref, k_ref, v_ref, szm_ref, o_ref, *, tq):
    qi = pl.program_id(2)
    row = lax.broadcasted_iota(jnp.int32, (tq, tq), 0)
    col = lax.broadcasted_iota(jnp.int32, (tq, tq), 1)
    causal = row >= col
    heads = []
    for h in range(2):
        q = q_ref[0, :, h * LANES:(h + 1) * LANES]

        def block(ki, carry, masked, h=h, q=q):
            m, l, acc = carry
            k0 = pl.multiple_of(ki * tq, tq)
            k = k_ref[0, pl.ds(k0, tq), h * LANES:(h + 1) * LANES]
            v = v_ref[0, pl.ds(k0, tq), :]
            s = _dot_nt(q, k)
            if masked:
                s = jnp.where(causal, s, -1e30)
            m_new = jnp.maximum(m, jnp.max(s, axis=1, keepdims=True))
            alpha = jnp.exp(m - m_new)
            p = jnp.exp(s - m_new)
            l = alpha * l + jnp.sum(p, axis=1, keepdims=True)
            acc = alpha * acc + _dot(p.astype(BF16), v)
            return m_new, l, acc

        init = (jnp.full((tq, 1), -1e30, F32), jnp.zeros((tq, 1), F32), jnp.zeros((tq, LANES), F32))
        carry = lax.fori_loop(0, qi, functools.partial(block, masked=False), init)
        _, l, acc = block(qi, carry, True)
        heads.append(acc / l)
    lane = lax.broadcasted_iota(jnp.int32, (tq, LANES), 1)
    o = jnp.where(lane < MLA_V, heads[0], heads[1]) * szm_ref[0].astype(F32)
    o_ref[0] = o.astype(BF16)


def _mla(q, k, v, szm, tq):
    b, s, _ = q.shape
    return pl.pallas_call(
        functools.partial(_mla_body, tq=tq),
        grid=(b, MLA_HEADS // 2, s // tq),
        in_specs=[pl.BlockSpec((1, tq, 2 * LANES), lambda bi, hp, qi: (bi, qi, hp)),
                  pl.BlockSpec((1, s, 2 * LANES), lambda bi, hp, qi: (bi, 0, hp)),
                  pl.BlockSpec((1, s, LANES), lambda bi, hp, qi: (bi, 0, hp)),
                  pl.BlockSpec((1, tq, LANES), lambda bi, hp, qi: (bi, qi, hp))],
        out_specs=pl.BlockSpec((1, tq, LANES), lambda bi, hp, qi: (bi, qi, hp)),
        out_shape=jax.ShapeDtypeStruct((b, s, MLA_WIDTH), BF16),
        compiler_params=_params(("parallel", "parallel", "arbitrary")),
        name="mla_attention",
    )(q, k, v, szm)


def _memattn_body(q_ref, kv_ref, szx_ref, o_ref):
    scale = X_HEAD_DIM ** -0.5
    for h in range(X_HEADS):
        sl = slice(h * X_HEAD_DIM, (h + 1) * X_HEAD_DIM)
        q = q_ref[0, :, sl]
        k = kv_ref[0, 0, :, sl]
        v = kv_ref[0, 0, :, X_WIDTH + h * X_HEAD_DIM:X_WIDTH + (h + 1) * X_HEAD_DIM]
        s = _dot_nt(q, k) * scale
        p = jnp.exp(s - jnp.max(s, axis=1, keepdims=True))
        o = _dot(p.astype(BF16), v) / jnp.sum(p, axis=1, keepdims=True)
        o_ref[0, :, sl] = (o * szx_ref[0, :, sl].astype(F32)).astype(BF16)


def _memattn(qx, kvm, szx, layer, tm):
    b, s, _ = qx.shape
    m = kvm.shape[2]
    tile = pl.BlockSpec((1, tm, X_WIDTH), lambda bi, i: (bi, i, 0))
    return pl.pallas_call(
        _memattn_body,
        grid=(b, s // tm),
        in_specs=[tile, pl.BlockSpec((1, 1, m, 2 * X_WIDTH), lambda bi, i: (layer, bi, 0, 0)), tile],
        out_specs=tile,
        out_shape=jax.ShapeDtypeStruct((b, s, X_WIDTH), BF16),
        compiler_params=_params(("parallel", "parallel")),
        name="mem_attention",
    )(qx, kvm, szx)


def _merge_body(x_ref, g_ref, szs_ref, ymla_ref, ymem_ref, wg_ref, bg_ref, wglu_ref, bglu_ref,
                pssm_ref, pmla_ref, pmem_ref, wout_ref, lng_ref, lnb_ref, o_ref, *, alpha):
    x = x_ref[...]
    xb = x.astype(BF16)
    t = _dot(g_ref[...], wglu_ref[0]) + bglu_ref[0]
    y_ssm = (t[:, :SSM_WIDTH] * _sigmoid(t[:, SSM_WIDTH:]) * szs_ref[...].astype(F32)).astype(BF16)
    merged = None
    for i, (y, p_ref) in enumerate(((y_ssm, pssm_ref), (ymla_ref[...], pmla_ref), (ymem_ref[...], pmem_ref))):
        cols = slice(i * D_MODEL, (i + 1) * D_MODEL)
        gate = _sigmoid(_dot(xb, wg_ref[0, :, cols]) + bg_ref[0, :, cols])
        term = gate * _dot(y, p_ref[0])
        merged = term if merged is None else merged + term
    r = alpha * x + _dot(merged.astype(BF16), wout_ref[0])
    mu = jnp.mean(r, axis=-1, keepdims=True)
    var = jnp.mean(jnp.square(r - mu), axis=-1, keepdims=True)
    o_ref[...] = (r - mu) * lax.rsqrt(var + NORM_EPS) * lng_ref[0] + lnb_ref[0]


def _merge(x, g, szs, ymla, ymem, wg, bg, wglu, bglu, pssm, pmla, pmem, wout, lng, lnb, layer, tm, alpha):
    t = x.shape[0]
    row = lambda w: pl.BlockSpec((tm, w), lambda i: (i, 0))
    lsel = lambda a: pl.BlockSpec((1,) + a.shape[1:], lambda i: (layer,) + (0,) * (a.ndim - 1))
    weights = (wg, bg, wglu, bglu, pssm, pmla, pmem, wout, lng, lnb)
    return pl.pallas_call(
        functools.partial(_merge_body, alpha=alpha),
        grid=(t // tm,),
        in_specs=[row(D_MODEL), row(SSM_WIDTH), row(SSM_WIDTH), row(MLA_WIDTH), row(X_WIDTH)]
        + [lsel(w) for w in weights],
        out_specs=row(D_MODEL),
        out_shape=jax.ShapeDtypeStruct((t, D_MODEL), F32),
        compiler_params=_params(("parallel",)),
        name="merge_out_ln",
    )(x, g, szs, ymla, ymem, *weights)


def _rotate_half_cols(w):
    half = w.shape[-1] // 2
    return jnp.concatenate([-w[..., half:], w[..., :half]], axis=-1)


def _pack_weights(w_in, w_uq, w_ukv):
    depth = w_in.shape[0]
    split = (SSM_WIDTH, SSM_WIDTH, MLA_Q_RANK, MLA_KV_RANK, MLA_ROPE, MLA_WIDTH, X_WIDTH, X_WIDTH)
    offs = [0]
    for w in split:
        offs.append(offs[-1] + w)
    seg = [w_in[:, :, offs[i]:offs[i + 1]] for i in range(len(split))]
    w_u, w_zs, w_cq, w_ckv, w_kr, w_zm, w_qx, w_zx = seg
    w_gate = w_in[:, :, offs[-1]:]

    def rope_tile(w):
        z = jnp.zeros(w.shape[:-1] + (MLA_NOPE,), w.dtype)
        z2 = jnp.zeros(w.shape[:-1] + (LANES - MLA_NOPE - MLA_ROPE,), w.dtype)
        return jnp.concatenate([z, w, z2], axis=-1)

    w1 = jnp.concatenate([w_u, w_zs, w_cq, w_ckv, rope_tile(w_kr), rope_tile(_rotate_half_cols(w_kr)),
                          w_zm, w_qx, w_zx], axis=-1).astype(BF16)

    uq = w_uq.reshape(depth, MLA_Q_RANK, MLA_HEADS, MLA_NOPE + MLA_ROPE)
    q_nope, q_rope = uq[..., :MLA_NOPE], uq[..., MLA_NOPE:]
    pad = jnp.zeros(q_rope.shape, uq.dtype)
    q_main = jnp.concatenate([q_nope, q_rope, pad], axis=-1)
    q_rot = jnp.concatenate([jnp.zeros(q_nope.shape, uq.dtype), _rotate_half_cols(q_rope), pad], axis=-1)
    wq = jnp.concatenate([q_main.reshape(depth, MLA_Q_RANK, _QW), q_rot.reshape(depth, MLA_Q_RANK, _QW)],
                         axis=-1).astype(BF16)

    ukv = w_ukv.reshape(depth, MLA_KV_RANK, MLA_HEADS, MLA_NOPE + MLA_V)
    k_nope, v = ukv[..., :MLA_NOPE], ukv[..., MLA_NOPE:]
    k_tile = jnp.concatenate([k_nope, jnp.zeros(k_nope.shape, ukv.dtype)], axis=-1)
    wkv = jnp.concatenate([k_tile.reshape(depth, MLA_KV_RANK, _QW), v.reshape(depth, MLA_KV_RANK, MLA_WIDTH)],
                          axis=-1).astype(BF16)
    return w1, w_gate.astype(BF16), wq, wkv


def _pack_ssm_params(a_re, a_im, log_dt, b_re, b_im, c_re, c_im, d):
    depth = a_re.shape[0]
    n = depth * SSM_GROUPS
    ar = a_re.reshape(n, SSM_STATE)
    ai = a_im.reshape(n, SSM_STATE)
    ldt = jnp.broadcast_to(log_dt.reshape(n, 1, 1), (n, 1, 2 * SSM_STATE))
    a2 = jnp.stack([jnp.concatenate([ar, ar], -1), jnp.concatenate([ai, ai], -1)], axis=1)
    acol = jnp.stack([ar, ai], axis=1)[..., None]
    bt = jnp.stack([b_re.reshape(n, SSM_STATE, SSM_GROUP), b_im.reshape(n, SSM_STATE, SSM_GROUP)], axis=1)
    bt = jnp.swapaxes(bt, -1, -2)
    bt2 = jnp.concatenate([bt, bt], axis=-1)
    ct = jnp.stack([c_re.reshape(n, SSM_GROUP, SSM_STATE), c_im.reshape(n, SSM_GROUP, SSM_STATE)], axis=1)
    ct = jnp.swapaxes(ct, -1, -2)
    ce = jnp.tile(ct, (1, 1, 1, SSM_CHUNK))
    d_e = jnp.tile(d.reshape(n, 1, SSM_GROUP), (1, 1, SSM_CHUNK))
    return ldt, a2, acol, bt2, ce, d_e


def kernel(x, mem, positions, w_in, b_gate, ssm_a_re, ssm_a_im, ssm_log_dt, ssm_b_re, ssm_b_im, ssm_c_re,
           ssm_c_im, ssm_d, w_glu, b_glu, mla_q_norm, w_uq, mla_kv_norm, w_ukv, w_mem_kv, p_ssm, p_mla, p_mem,
           w_out, ln_g, ln_b):
    bsz, s, d = x.shape
    depth = w_in.shape[0]
    mlen = mem.shape[1]
    t = bsz * s
    assert d == D_MODEL and s % SSM_CHUNK == 0 and SSM_GROUPS % SSM_GB == 0
    tm = min(512, s)
    tq = min(512, s)
    assert s % tm == 0 and s % tq == 0
    alpha = (2 * depth) ** 0.25
    n_chunks = s // SSM_CHUNK

    w1, w_gate, wq, wkv = _pack_weights(w_in, w_uq, w_ukv)
    row3 = lambda a: a.reshape(depth, 1, a.shape[-1])
    qn, kvn = row3(mla_q_norm), row3(mla_kv_norm)
    bg, bglu, lng, lnb = row3(b_gate), row3(b_glu), row3(ln_g), row3(ln_b)
    wglu, pssm, pmla, pmem, wout = (a.astype(BF16) for a in (w_glu, p_ssm, p_mla, p_mem, w_out))

    inv_freq = ROPE_THETA ** (-jnp.arange(0, MLA_ROPE, 2, dtype=F32) / MLA_ROPE)
    invf = jnp.concatenate([jnp.zeros((MLA_NOPE,), F32), inv_freq, inv_freq,
                            jnp.zeros((LANES - MLA_NOPE - MLA_ROPE,), F32)]).reshape(1, LANES)
    pos_b = jnp.broadcast_to(positions.astype(F32).reshape(t, 1), (t, LANES))
    cos, sin = _rope_tables(pos_b, invf, tm)

    kvm = _memkv(mem.reshape(bsz * mlen, d).astype(BF16), w_mem_kv.astype(BF16))
    kvm = kvm.reshape(depth, bsz, mlen, 2 * X_WIDTH)

    ldt, a2, acol, bt2, ce, d_e = _pack_ssm_params(ssm_a_re, ssm_a_im, ssm_log_dt, ssm_b_re, ssm_b_im,
                                                   ssm_c_re, ssm_c_im, ssm_d)
    toep, bs, cs, al = _ssm_prep(ldt, a2, acol, bt2, ce)

    xf = x.reshape(t, d)
    for layer in range(depth):
        u, szs, q, k, v, szm, qx, szx = _inproj(xf, w1, wq, wkv, qn, kvn, cos, sin, layer, tm)

        u_g = u.reshape(bsz, n_chunks, SSM_CHUNK, SSM_GROUPS, SSM_GROUP).transpose(3, 1, 0, 2, 4)
        u_g = u_g.reshape(SSM_GROUPS, n_chunks * bsz, SSM_LC)
        g_g = _ssm(u_g, toep, bs, cs, al, d_e, layer, bsz)
        g = g_g.reshape(SSM_GROUPS, n_chunks, bsz, SSM_CHUNK, SSM_GROUP).transpose(2, 1, 3, 0, 4)
        g = g.reshape(t, SSM_WIDTH)

        y_mla = _mla(q.reshape(bsz, s, _QW), k.reshape(bsz, s, _QW), v.reshape(bsz, s, MLA_WIDTH),
                     szm.reshape(bsz, s, MLA_WIDTH), tq)
        y_mem = _memattn(qx.reshape(bsz, s, X_WIDTH), kvm, szx.reshape(bsz, s, X_WIDTH), layer, tm)

        xf = _merge(xf, g, szs, y_mla.reshape(t, MLA_WIDTH), y_mem.reshape(t, X_WIDTH), w_gate, bg, wglu, bglu,
                    pssm, pmla, pmem, wout, lng, lnb, layer, tm, alpha)
    return xf.reshape(bsz, s, d)
```

```python
import functools
import math

import jax
import jax.numpy as jnp
from jax import lax
from jax.experimental import pallas as pl
from jax.experimental.pallas import tpu as pltpu

F32 = jnp.float32
BF16 = jnp.bfloat16

LANES = 128
D_MODEL = 1024
SSM_WIDTH = 512
SSM_GROUP = 16
SSM_GROUPS = SSM_WIDTH // SSM_GROUP
SSM_STATE = 64
MLA_HEADS = 8
MLA_NOPE = 64
MLA_ROPE = 32
MLA_V = 64
MLA_Q_RANK = 256
MLA_KV_RANK = 128
MLA_WIDTH = MLA_HEADS * MLA_V
ROPE_THETA = 10000.0
X_HEADS = 4
X_HEAD_DIM = 128
X_WIDTH = X_HEADS * X_HEAD_DIM
N_BRANCH = 3
NORM_EPS = 1e-5

SSM_CHUNK = 32
SSM_LC = SSM_CHUNK * SSM_GROUP
SSM_GB = 4
MLA_HB = 4
VMEM_LIMIT = 48 * 1024 * 1024


def _sigmoid(v):
    return jax.nn.sigmoid(v)


def _silu(v):
    return v * jax.nn.sigmoid(v)


def _dot(a, b):
    return jnp.dot(a, b, preferred_element_type=F32)


def _dot_nt(a, b):
    return lax.dot_general(a, b, (((1,), (1,)), ((), ())), preferred_element_type=F32)


def _params(sem):
    return pltpu.CompilerParams(dimension_semantics=sem, vmem_limit_bytes=VMEM_LIMIT)


def _rope_body(pos_ref, invf_ref, c_ref, s_ref):
    ang = pos_ref[...] * invf_ref[...]
    lane = lax.broadcasted_iota(jnp.int32, ang.shape, 1)
    rope = (lane >= MLA_NOPE) & (lane < MLA_NOPE + MLA_ROPE)
    c_ref[...] = jnp.where(lane < MLA_NOPE, 1.0, jnp.where(rope, jnp.cos(ang), 0.0))
    s_ref[...] = jnp.where(rope, jnp.sin(ang), 0.0)


def _rope_tables(pos_b, invf, tm):
    t = pos_b.shape[0]
    spec = pl.BlockSpec((tm, LANES), lambda i: (i, 0))
    return pl.pallas_call(
        _rope_body,
        grid=(t // tm,),
        in_specs=[spec, pl.BlockSpec((1, LANES), lambda i: (0, 0))],
        out_specs=[spec, spec],
        out_shape=[jax.ShapeDtypeStruct((t, LANES), F32)] * 2,
        compiler_params=_params(("parallel",)),
        name="rope_tables",
    )(pos_b, invf)


def _memkv_body(mem_ref, w_ref, o_ref):
    o_ref[0] = _dot(mem_ref[...], w_ref[0]).astype(BF16)


def _memkv(mem_b, w_b):
    depth = w_b.shape[0]
    rows = mem_b.shape[0]
    return pl.pallas_call(
        _memkv_body,
        grid=(depth,),
        in_specs=[pl.BlockSpec((rows, D_MODEL), lambda l: (0, 0)),
                  pl.BlockSpec((1, D_MODEL, 2 * X_WIDTH), lambda l: (l, 0, 0))],
        out_specs=pl.BlockSpec((1, rows, 2 * X_WIDTH), lambda l: (l, 0, 0)),
        out_shape=jax.ShapeDtypeStruct((depth, rows, 2 * X_WIDTH), BF16),
        compiler_params=_params(("parallel",)),
        name="mem_kv",
    )(mem_b, w_b)


_C_U = 0
_C_ZS = _C_U + SSM_WIDTH
_C_CQ = _C_ZS + SSM_WIDTH
_C_CKV = _C_CQ + MLA_Q_RANK
_C_KR = _C_CKV + MLA_KV_RANK
_C_ZM = _C_KR + 2 * LANES
_C_QX = _C_ZM + MLA_WIDTH
_C_ZX = _C_QX + X_WIDTH
_C_END = _C_ZX + X_WIDTH
_QW = MLA_HEADS * LANES


def _rms(v, g):
    return v * lax.rsqrt(jnp.mean(jnp.square(v), axis=-1, keepdims=True) + NORM_EPS) * g


def _inproj_body(x_ref, w1_ref, wq_ref, wkv_ref, qn_ref, kvn_ref, c_ref, s_ref,
                 u_ref, szs_ref, q_ref, k_ref, v_ref, szm_ref, qx_ref, szx_ref, u_scr, ug_scr):
    xb = x_ref[...].astype(BF16)

    def proj(c0, c1):
        return _dot(xb, w1_ref[0, :, c0:c1])

    u = proj(_C_U, _C_ZS)
    n_c = u_scr.shape[1] // SSM_CHUNK
    gpt = LANES // SSM_GROUP
    for j in range(SSM_WIDTH // LANES):
        u_scr[j] = u[:, j * LANES:(j + 1) * LANES]
    for s in range(SSM_CHUNK):
        for j in range(SSM_WIDTH // LANES):
            a = u_scr[j, pl.ds(s, n_c, stride=SSM_CHUNK), :]
            for g in range(gpt):
                ug_scr[j * gpt + g, :, s * SSM_GROUP:(s + 1) * SSM_GROUP] = a[:, g * SSM_GROUP:(g + 1) * SSM_GROUP]
    u_ref[...] = ug_scr[...].astype(BF16)
    szs_ref[...] = _silu(proj(_C_ZS, _C_CQ)).astype(BF16)
    szm_ref[...] = _silu(proj(_C_ZM, _C_QX)).astype(BF16)
    qx_ref[...] = proj(_C_QX, _C_ZX).astype(BF16)
    szx_ref[...] = _silu(proj(_C_ZX, _C_END)).astype(BF16)

    cos = c_ref[...]
    sin = s_ref[...]
    scale = (MLA_NOPE + MLA_ROPE) ** -0.5 * math.log2(math.e)
    cos_q = cos * scale
    sin_q = sin * scale

    cq = _rms(proj(_C_CQ, _C_CKV), qn_ref[0]).astype(BF16)
    qa = _dot(cq, wq_ref[0])
    for h in range(MLA_HEADS):
        a = qa[:, h * LANES:(h + 1) * LANES]
        b = qa[:, _QW + h * LANES:_QW + (h + 1) * LANES]
        q_ref[:, h * LANES:(h + 1) * LANES] = (a * cos_q + b * sin_q).astype(BF16)

    ckv = _rms(proj(_C_CKV, _C_KR), kvn_ref[0]).astype(BF16)
    kv = _dot(ckv, wkv_ref[0])
    kr = proj(_C_KR, _C_ZM)
    kr = kr[:, :LANES] * cos + kr[:, LANES:] * sin
    lane = lax.broadcasted_iota(jnp.int32, (1, LANES), 1)
    one_hot = jnp.where(lane == MLA_V, 1.0, 0.0)
    for h in range(MLA_HEADS):
        k_ref[:, h * LANES:(h + 1) * LANES] = (kv[:, h * LANES:(h + 1) * LANES] + kr).astype(BF16)
        v_ref[:, h * LANES:(h + 1) * LANES] = (kv[:, _QW + h * LANES:_QW + (h + 1) * LANES] + one_hot).astype(BF16)


def _inproj(x, w1, wq, wkv, qn, kvn, cos, sin, layer, tm):
    t = x.shape[0]
    n_c = tm // SSM_CHUNK
    row = lambda w: pl.BlockSpec((tm, w), lambda i: (i, 0))
    lsel = lambda a: pl.BlockSpec((1,) + a.shape[1:], lambda i: (layer,) + (0,) * (a.ndim - 1))
    widths = (SSM_WIDTH, _QW, _QW, _QW, MLA_WIDTH, X_WIDTH, X_WIDTH)
    slab = (SSM_GROUPS, n_c, SSM_LC)
    return pl.pallas_call(
        _inproj_body,
        grid=(t // tm,),
        in_specs=[row(D_MODEL), lsel(w1), lsel(wq), lsel(wkv), lsel(qn), lsel(kvn), row(LANES), row(LANES)],
        out_specs=[pl.BlockSpec(slab, lambda i: (0, i, 0))] + [row(w) for w in widths],
        out_shape=[jax.ShapeDtypeStruct((SSM_GROUPS, t // SSM_CHUNK, SSM_LC), BF16)]
        + [jax.ShapeDtypeStruct((t, w), BF16) for w in widths],
        scratch_shapes=[pltpu.VMEM((SSM_WIDTH // LANES, tm, LANES), F32), pltpu.VMEM(slab, F32)],
        compiler_params=_params(("parallel",)),
        name="in_proj",
    )(x, w1, wq, wkv, qn, kvn, cos, sin)


def _ssm_prep_body(ldt_ref, a2_ref, acol_ref, bt2_ref, ce_ref,
                   toep_ref, bs_ref, cs_ref, al_ref):
    chunk = SSM_CHUNK
    dt = jnp.exp(ldt_ref[0])
    lr2 = a2_ref[0, 0:1, :]
    li2 = a2_ref[0, 1:2, :]
    lane2 = lax.broadcasted_iota(jnp.int32, (1, 2 * SSM_STATE), 1)
    first = lane2 < SSM_STATE

    mag = jnp.exp(lr2 * dt)
    lb_re = mag * jnp.cos(li2 * dt)
    lb_im = mag * jnp.sin(li2 * dt)
    nr, ni = lb_re - 1.0, lb_im
    den = lr2 * lr2 + li2 * li2
    f_re = (nr * lr2 + ni * li2) / den
    f_im = (ni * lr2 - nr * li2) / den
    bt_re = bt2_ref[0, 0]
    bt_im = bt2_ref[0, 1]
    bb_re = f_re * bt_re - f_im * bt_im
    bb_im = f_re * bt_im + f_im * bt_re

    srow = lax.broadcasted_iota(jnp.int32, (SSM_LC, 1), 0) // SSM_GROUP
    e = (chunk - 1 - srow).astype(F32)
    pm = jnp.exp(lr2 * dt * e)
    pr = pm * jnp.cos(li2 * dt * e)
    pi = pm * jnp.sin(li2 * dt * e)
    br = jnp.tile(bb_re, (chunk, 1))
    bi = jnp.tile(bb_im, (chunk, 1))
    s_re = pr * br - pi * bi
    s_im = pr * bi + pi * br
    bs_ref[0, :, :2 * SSM_STATE] = jnp.where(first, s_re, s_im).astype(BF16)
    bs_ref[0, :, 2 * SSM_STATE:] = jnp.where(first, s_im, s_re).astype(BF16)

    lr = acol_ref[0, 0]
    li = acol_ref[0, 1]
    dtc = dt[:, 0:1]
    kk = (lax.broadcasted_iota(jnp.int32, (1, SSM_LC), 1) // SSM_GROUP).astype(F32)
    c_re = ce_ref[0, 0]
    c_im = ce_ref[0, 1]

    def c_times_power(kf):
        m = jnp.exp(lr * dtc * kf)
        p_re = m * jnp.cos(li * dtc * kf)
        p_im = m * jnp.sin(li * dtc * kf)
        return c_re * p_re - c_im * p_im, c_re * p_im + c_im * p_re

    m_re, m_im = c_times_power(kk)
    bcol_re = bb_re[:, :SSM_STATE]
    bcol_im = bb_im[:, :SSM_STATE]
    hi = lax.Precision.HIGHEST
    kcat = (jnp.dot(bcol_re, m_re, precision=hi, preferred_element_type=F32)
            - jnp.dot(bcol_im, m_im, precision=hi, preferred_element_type=F32))
    lane = lax.broadcasted_iota(jnp.int32, (SSM_GROUP, SSM_LC), 1)
    for s in range(chunk):
        shifted = kcat if s == 0 else pltpu.roll(kcat, SSM_GROUP * s, 1)
        toep_ref[0, s * SSM_GROUP:(s + 1) * SSM_GROUP, :] = jnp.where(
            lane >= SSM_GROUP * s, shifted, 0.0).astype(BF16)

    ab_m = jnp.exp(lr * dtc)
    ab_re = ab_m * jnp.cos(li * dtc)
    ab_im = ab_m * jnp.sin(li * dtc)
    n_re = m_re * ab_re - m_im * ab_im
    n_im = m_re * ab_im + m_im * ab_re
    cs_ref[0, :SSM_STATE, :] = n_re.astype(BF16)
    cs_ref[0, SSM_STATE:, :] = (-n_im).astype(BF16)

    lf = float(chunk)
    am = jnp.exp(lr2 * dt * lf)
    a_re = am * jnp.cos(li2 * dt * lf)
    a_im = am * jnp.sin(li2 * dt * lf)
    al_ref[0, 0:1, :] = a_re
    al_ref[0, 1:2, :] = jnp.where(first, -a_im, a_im)


def _ssm_prep(ldt, a2, acol, bt2, ce):
    n = ldt.shape[0]
    sel = lambda a: pl.BlockSpec((1,) + a.shape[1:], lambda i: (i,) + (0,) * (a.ndim - 1))
    out_shapes = [jax.ShapeDtypeStruct((n, SSM_LC, SSM_LC), BF16),
                  jax.ShapeDtypeStruct((n, SSM_LC, 4 * SSM_STATE), BF16),
                  jax.ShapeDtypeStruct((n, 2 * SSM_STATE, SSM_LC), BF16),
                  jax.ShapeDtypeStruct((n, 2, 2 * SSM_STATE), F32)]
    return pl.pallas_call(
        _ssm_prep_body,
        grid=(n,),
        in_specs=[sel(ldt), sel(a2), sel(acol), sel(bt2), sel(ce)],
        out_specs=[sel(s) for s in out_shapes],
        out_shape=out_shapes,
        compiler_params=_params(("parallel",)),
        name="ssm_prep",
    )(ldt, a2, acol, bt2, ce)


def _ssm_body(u_ref, toep_ref, bs_ref, cs_ref, al_ref, d_ref, o_ref, hloc_ref, hst_ref, *, batch, n_chunks):
    gb = u_ref.shape[0]
    w = 2 * SSM_STATE
    for j in range(gb):
        hl = _dot(u_ref[j], bs_ref[j])
        hloc_ref[j, 0] = hl[:, :w]
        hloc_ref[j, 1] = hl[:, w:]

    a1 = [jnp.broadcast_to(al_ref[j, 0:1, :], (batch, 2 * SSM_STATE)) for j in range(gb)]
    a2 = [jnp.broadcast_to(al_ref[j, 1:2, :], (batch, 2 * SSM_STATE)) for j in range(gb)]

    def step(c, carry):
        out = []
        rows = pl.ds(c, batch, stride=n_chunks)
        for j in range(gb):
            h, hs = carry[j]
            hst_ref[j, rows, :] = h
            out.append((a1[j] * h + a2[j] * hs + hloc_ref[j, 0, rows, :],
                        a1[j] * hs - a2[j] * h + hloc_ref[j, 1, rows, :]))
        return tuple(out)

    zero = jnp.zeros((batch, 2 * SSM_STATE), F32)
    lax.fori_loop(0, n_chunks, step, tuple((zero, zero) for _ in range(gb)))

    for j in range(gb):
        u = u_ref[j]
        y = (_dot(u, toep_ref[j]) + _dot(hst_ref[j].astype(BF16), cs_ref[j])
             + d_ref[j] * u.astype(F32))
        o_ref[j] = jax.nn.gelu(y).astype(BF16)


def _ssm(u_g, toep, bs, cs, al, d_e, layer, batch):
    groups, rows, _ = u_g.shape
    n_chunks = rows // batch
    gb = SSM_GB
    lsel = lambda a: pl.BlockSpec((gb,) + a.shape[1:],
                                  lambda i: (layer * (groups // gb) + i,) + (0,) * (a.ndim - 1))
    useg = pl.BlockSpec((gb, rows, SSM_LC), lambda i: (i, 0, 0))
    return pl.pallas_call(
        functools.partial(_ssm_body, batch=batch, n_chunks=n_chunks),
        grid=(groups // gb,),
        in_specs=[useg, lsel(toep), lsel(bs), lsel(cs), lsel(al), lsel(d_e)],
        out_specs=useg,
        out_shape=jax.ShapeDtypeStruct(u_g.shape, BF16),
        scratch_shapes=[pltpu.VMEM((gb, 2, rows, 2 * SSM_STATE), F32),
                        pltpu.VMEM((gb, rows, 2 * SSM_STATE), F32)],
        compiler_params=_params(("parallel",)),
        name="s5_scan",
    )(u_g, toep, bs, cs, al, d_e)


def _mla_body(q_ref, k_ref, v_ref, szm_ref, o_ref, *, tq):
    qi = pl.program_id(2)
    row = lax.broadcasted_iota(jnp.int32, (tq, tq), 0)
    col = lax.broadcasted_iota(jnp.int32, (tq, tq), 1)
    causal = row >= col
    qs = [q_ref[0, :, h * LANES:(h + 1) * LANES] for h in range(MLA_HB)]

    def block(ki, carry, masked):
        k0 = pl.multiple_of(ki * tq, tq)
        out = []
        for h in range(MLA_HB):
            m, acc = carry[h]
            k = k_ref[0, pl.ds(k0, tq), h * LANES:(h + 1) * LANES]
            v = v_ref[0, pl.ds(k0, tq), h * LANES:(h + 1) * LANES]
            s = _dot_nt(qs[h], k)
            if masked:
                s = jnp.where(causal, s, -1e30)
            m_new = jnp.maximum(m, jnp.max(s, axis=1, keepdims=True))
            alpha = jnp.exp2(m - m_new)
            p = jnp.exp2((s - m_new).astype(BF16))
            out.append((m_new, alpha * acc + _dot(p, v)))
        return tuple(out)

    init = tuple((jnp.full((tq, 1), -1e30, F32), jnp.zeros((tq, LANES), F32)) for _ in range(MLA_HB))
    carry = lax.fori_loop(0, qi, functools.partial(block, masked=False), init)
    carry = block(qi, carry, True)

    lane = lax.broadcasted_iota(jnp.int32, (tq, LANES), 1)
    for hp in range(MLA_HB // 2):
        o0, o1 = (carry[2 * hp + i][1] for i in range(2))
        o0 = o0 / o0[:, MLA_V:MLA_V + 1]
        o1 = o1 / o1[:, MLA_V:MLA_V + 1]
        o = jnp.where(lane < MLA_V, o0, pltpu.roll(o1, MLA_V, 1))
        sl = slice(hp * LANES, (hp + 1) * LANES)
        o_ref[0, :, sl] = (o * szm_ref[0, :, sl].astype(F32)).astype(BF16)


def _mla(q, k, v, szm, tq):
    b, s, _ = q.shape
    hb = MLA_HB
    qkv = lambda rows, blk: pl.BlockSpec((1, rows, hb * LANES), blk)
    outw = hb * MLA_V
    return pl.pallas_call(
        functools.partial(_mla_body, tq=tq),
        grid=(b, MLA_HEADS // hb, s // tq),
        in_specs=[qkv(tq, lambda bi, hp, qi: (bi, qi, hp)),
                  qkv(s, lambda bi, hp, qi: (bi, 0, hp)),
                  qkv(s, lambda bi, hp, qi: (bi, 0, hp)),
                  pl.BlockSpec((1, tq, outw), lambda bi, hp, qi: (bi, qi, hp))],
        out_specs=pl.BlockSpec((1, tq, outw), lambda bi, hp, qi: (bi, qi, hp)),
        out_shape=jax.ShapeDtypeStruct((b, s, MLA_WIDTH), BF16),
        compiler_params=_params(("parallel", "parallel", "arbitrary")),
        name="mla_attention",
    )(q, k, v, szm)


def _memattn_body(q_ref, kv_ref, szx_ref, o_ref):
    scale = X_HEAD_DIM ** -0.5
    for h in range(X_HEADS):
        sl = slice(h * X_HEAD_DIM, (h + 1) * X_HEAD_DIM)
        q = q_ref[0, :, sl]
        k = kv_ref[0, 0, :, sl]
        v = kv_ref[0, 0, :, X_WIDTH + h * X_HEAD_DIM:X_WIDTH + (h + 1) * X_HEAD_DIM]
        s = _dot_nt(q, k) * scale
        p = jnp.exp(s - jnp.max(s, axis=1, keepdims=True))
        o = _dot(p.astype(BF16), v) / jnp.sum(p, axis=1, keepdims=True)
        o_ref[0, :, sl] = (o * szx_ref[0, :, sl].astype(F32)).astype(BF16)


def _memattn(qx, kvm, szx, layer, tm):
    b, s, _ = qx.shape
    m = kvm.shape[2]
    tile = pl.BlockSpec((1, tm, X_WIDTH), lambda bi, i: (bi, i, 0))
    return pl.pallas_call(
        _memattn_body,
        grid=(b, s // tm),
        in_specs=[tile, pl.BlockSpec((1, 1, m, 2 * X_WIDTH), lambda bi, i: (layer, bi, 0, 0)), tile],
        out_specs=tile,
        out_shape=jax.ShapeDtypeStruct((b, s, X_WIDTH), BF16),
        compiler_params=_params(("parallel", "parallel")),
        name="mem_attention",
    )(qx, kvm, szx)


def _merge_body(x_ref, g_ref, szs_ref, ymla_ref, ymem_ref, wg_ref, bg_ref, wglu_ref, bglu_ref,
                pssm_ref, pmla_ref, pmem_ref, wout_ref, lng_ref, lnb_ref, o_ref, st_scr, gt_scr, *, alpha):
    x = x_ref[...]
    xb = x.astype(BF16)
    n_c = g_ref.shape[1]
    for g in range(SSM_GROUPS):
        blk = g_ref[g].astype(F32)
        for s in range(SSM_CHUNK):
            st_scr[s, :, g * SSM_GROUP:(g + 1) * SSM_GROUP] = blk[:, s * SSM_GROUP:(s + 1) * SSM_GROUP]
    tiles = SSM_WIDTH // LANES
    for s in range(SSM_CHUNK):
        for j in range(tiles):
            gt_scr[j, pl.ds(s, n_c, stride=SSM_CHUNK), :] = st_scr[s, :, j * LANES:(j + 1) * LANES]
    gt = jnp.concatenate([gt_scr[j] for j in range(tiles)], axis=1).astype(BF16)
    t = _dot(gt, wglu_ref[0]) + bglu_ref[0]
    y_ssm = (t[:, :SSM_WIDTH] * _sigmoid(t[:, SSM_WIDTH:]) * szs_ref[...].astype(F32)).astype(BF16)
    merged = None
    for i, (y, p_ref) in enumerate(((y_ssm, pssm_ref), (ymla_ref[...], pmla_ref), (ymem_ref[...], pmem_ref))):
        cols = slice(i * D_MODEL, (i + 1) * D_MODEL)
        gate = _sigmoid(_dot(xb, wg_ref[0, :, cols]) + bg_ref[0, :, cols])
        term = gate * _dot(y, p_ref[0])
        merged = term if merged is None else merged + term
    r = alpha * x + _dot(merged.astype(BF16), wout_ref[0])
    mu = jnp.mean(r, axis=-1, keepdims=True)
    var = jnp.mean(jnp.square(r - mu), axis=-1, keepdims=True)
    o_ref[...] = (r - mu) * lax.rsqrt(var + NORM_EPS) * lng_ref[0] + lnb_ref[0]


def _merge(x, g, szs, ymla, ymem, wg, bg, wglu, bglu, pssm, pmla, pmem, wout, lng, lnb, layer, tm, alpha):
    t = x.shape[0]
    row = lambda w: pl.BlockSpec((tm, w), lambda i: (i, 0))
    lsel = lambda a: pl.BlockSpec((1,) + a.shape[1:], lambda i: (layer,) + (0,) * (a.ndim - 1))
    weights = (wg, bg, wglu, bglu, pssm, pmla, pmem, wout, lng, lnb)
    n_c = tm // SSM_CHUNK
    return pl.pallas_call(
        functools.partial(_merge_body, alpha=alpha),
        grid=(t // tm,),
        in_specs=[row(D_MODEL), pl.BlockSpec((SSM_GROUPS, n_c, SSM_LC), lambda i: (0, i, 0)),
                  row(SSM_WIDTH), row(MLA_WIDTH), row(X_WIDTH)]
        + [lsel(w) for w in weights],
        out_specs=row(D_MODEL),
        out_shape=jax.ShapeDtypeStruct((t, D_MODEL), F32),
        scratch_shapes=[pltpu.VMEM((SSM_CHUNK, n_c, SSM_WIDTH), F32),
                        pltpu.VMEM((SSM_WIDTH // LANES, tm, LANES), F32)],
        compiler_params=_params(("parallel",)),
        name="merge_out_ln",
    )(x, g, szs, ymla, ymem, *weights)


def _rotate_half_cols(w):
    half = w.shape[-1] // 2
    return jnp.concatenate([-w[..., half:], w[..., :half]], axis=-1)


def _pack_weights(w_in, w_uq, w_ukv):
    depth = w_in.shape[0]
    split = (SSM_WIDTH, SSM_WIDTH, MLA_Q_RANK, MLA_KV_RANK, MLA_ROPE, MLA_WIDTH, X_WIDTH, X_WIDTH)
    offs = [0]
    for w in split:
        offs.append(offs[-1] + w)
    seg = [w_in[:, :, offs[i]:offs[i + 1]] for i in range(len(split))]
    w_u, w_zs, w_cq, w_ckv, w_kr, w_zm, w_qx, w_zx = seg
    w_gate = w_in[:, :, offs[-1]:]

    def rope_tile(w):
        z = jnp.zeros(w.shape[:-1] + (MLA_NOPE,), w.dtype)
        z2 = jnp.zeros(w.shape[:-1] + (LANES - MLA_NOPE - MLA_ROPE,), w.dtype)
        return jnp.concatenate([z, w, z2], axis=-1)

    w1 = jnp.concatenate([w_u, w_zs, w_cq, w_ckv, rope_tile(w_kr), rope_tile(_rotate_half_cols(w_kr)),
                          w_zm, w_qx, w_zx], axis=-1).astype(BF16)

    uq = w_uq.reshape(depth, MLA_Q_RANK, MLA_HEADS, MLA_NOPE + MLA_ROPE)
    q_nope, q_rope = uq[..., :MLA_NOPE], uq[..., MLA_NOPE:]
    pad = jnp.zeros(q_rope.shape, uq.dtype)
    q_main = jnp.concatenate([q_nope, q_rope, pad], axis=-1)
    q_rot = jnp.concatenate([jnp.zeros(q_nope.shape, uq.dtype), _rotate_half_cols(q_rope), pad], axis=-1)
    wq = jnp.concatenate([q_main.reshape(depth, MLA_Q_RANK, _QW), q_rot.reshape(depth, MLA_Q_RANK, _QW)],
                         axis=-1).astype(BF16)

    ukv = w_ukv.reshape(depth, MLA_KV_RANK, MLA_HEADS, MLA_NOPE + MLA_V)
    k_nope, v = ukv[..., :MLA_NOPE], ukv[..., MLA_NOPE:]
    k_tile = jnp.concatenate([k_nope, jnp.zeros(k_nope.shape, ukv.dtype)], axis=-1)
    v_tile = jnp.concatenate([v, jnp.zeros(v.shape, ukv.dtype)], axis=-1)
    wkv = jnp.concatenate([k_tile.reshape(depth, MLA_KV_RANK, _QW), v_tile.reshape(depth, MLA_KV_RANK, _QW)],
                          axis=-1).astype(BF16)
    return w1, w_gate.astype(BF16), wq, wkv


def _pack_ssm_params(a_re, a_im, log_dt, b_re, b_im, c_re, c_im, d):
    depth = a_re.shape[0]
    n = depth * SSM_GROUPS
    ar = a_re.reshape(n, SSM_STATE)
    ai = a_im.reshape(n, SSM_STATE)
    ldt = jnp.broadcast_to(log_dt.reshape(n, 1, 1), (n, 1, 2 * SSM_STATE))
    a2 = jnp.stack([jnp.concatenate([ar, ar], -1), jnp.concatenate([ai, ai], -1)], axis=1)
    acol = jnp.stack([ar, ai], axis=1)[..., None]
    bt = jnp.stack([b_re.reshape(n, SSM_STATE, SSM_GROUP), b_im.reshape(n, SSM_STATE, SSM_GROUP)], axis=1)
    bt = jnp.swapaxes(bt, -1, -2)
    bt2 = jnp.concatenate([bt, bt], axis=-1)
    ct = jnp.stack([c_re.reshape(n, SSM_GROUP, SSM_STATE), c_im.reshape(n, SSM_GROUP, SSM_STATE)], axis=1)
    ct = jnp.swapaxes(ct, -1, -2)
    ce = jnp.tile(ct, (1, 1, 1, SSM_CHUNK))
    d_e = jnp.tile(d.reshape(n, 1, SSM_GROUP), (1, 1, SSM_CHUNK))
    return ldt, a2, acol, bt2, ce, d_e


def kernel(x, mem, positions, w_in, b_gate, ssm_a_re, ssm_a_im, ssm_log_dt, ssm_b_re, ssm_b_im, ssm_c_re,
           ssm_c_im, ssm_d, w_glu, b_glu, mla_q_norm, w_uq, mla_kv_norm, w_ukv, w_mem_kv, p_ssm, p_mla, p_mem,
           w_out, ln_g, ln_b):
    bsz, s, d = x.shape
    depth = w_in.shape[0]
    mlen = mem.shape[1]
    t = bsz * s
    assert d == D_MODEL and s % SSM_CHUNK == 0 and SSM_GROUPS % SSM_GB == 0
    tm = min(512, s)
    tq = min(512, s)
    assert s % tm == 0 and s % tq == 0
    alpha = (2 * depth) ** 0.25
    n_chunks = s // SSM_CHUNK

    w1, w_gate, wq, wkv = _pack_weights(w_in, w_uq, w_ukv)
    row3 = lambda a: a.reshape(depth, 1, a.shape[-1])
    qn, kvn = row3(mla_q_norm), row3(mla_kv_norm)
    bg, bglu, lng, lnb = row3(b_gate), row3(b_glu), row3(ln_g), row3(ln_b)
    wglu, pssm, pmla, pmem, wout = (a.astype(BF16) for a in (w_glu, p_ssm, p_mla, p_mem, w_out))

    inv_freq = ROPE_THETA ** (-jnp.arange(0, MLA_ROPE, 2, dtype=F32) / MLA_ROPE)
    invf = jnp.concatenate([jnp.zeros((MLA_NOPE,), F32), inv_freq, inv_freq,
                            jnp.zeros((LANES - MLA_NOPE - MLA_ROPE,), F32)]).reshape(1, LANES)
    pos_b = jnp.broadcast_to(positions.astype(F32).reshape(t, 1), (t, LANES))
    cos, sin = _rope_tables(pos_b, invf, tm)

    kvm = _memkv(mem.reshape(bsz * mlen, d).astype(BF16), w_mem_kv.astype(BF16))
    kvm = kvm.reshape(depth, bsz, mlen, 2 * X_WIDTH)

    ldt, a2, acol, bt2, ce, d_e = _pack_ssm_params(ssm_a_re, ssm_a_im, ssm_log_dt, ssm_b_re, ssm_b_im,
                                                   ssm_c_re, ssm_c_im, ssm_d)
    toep, bs, cs, al = _ssm_prep(ldt, a2, acol, bt2, ce)

    xf = x.reshape(t, d)
    for layer in range(depth):
        u_g, szs, q, k, v, szm, qx, szx = _inproj(xf, w1, wq, wkv, qn, kvn, cos, sin, layer, tm)
        g = _ssm(u_g, toep, bs, cs, al, d_e, layer, bsz)
        y_mla = _mla(q.reshape(bsz, s, _QW), k.reshape(bsz, s, _QW), v.reshape(bsz, s, _QW),
                     szm.reshape(bsz, s, MLA_WIDTH), tq)
        y_mem = _memattn(qx.reshape(bsz, s, X_WIDTH), kvm, szx.reshape(bsz, s, X_WIDTH), layer, tm)

        xf = _merge(xf, g, szs, y_mla.reshape(t, MLA_WIDTH), y_mem.reshape(t, X_WIDTH), w_gate, bg, wglu, bglu,
                    pssm, pmla, pmem, wout, lng, lnb, layer, tm, alpha)
    return xf.reshape(bsz, s, d)
```

```python
import functools
import math

import jax
import jax.numpy as jnp
from jax import lax
from jax.experimental import pallas as pl
from jax.experimental.pallas import tpu as pltpu

F32 = jnp.float32
BF16 = jnp.bfloat16

LANES = 128
D_MODEL = 1024
SSM_WIDTH = 512
SSM_GROUP = 16
SSM_GROUPS = SSM_WIDTH // SSM_GROUP
SSM_STATE = 64
MLA_HEADS = 8
MLA_NOPE = 64
MLA_ROPE = 32
MLA_V = 64
MLA_Q_RANK = 256
MLA_KV_RANK = 128
MLA_WIDTH = MLA_HEADS * MLA_V
ROPE_THETA = 10000.0
X_HEADS = 4
X_HEAD_DIM = 128
X_WIDTH = X_HEADS * X_HEAD_DIM
N_BRANCH = 3
NORM_EPS = 1e-5

SSM_CHUNK = 32
SSM_LC = SSM_CHUNK * SSM_GROUP
SSM_GB = 4
MLA_HB = 4
VMEM_LIMIT = 48 * 1024 * 1024


def _sigmoid(v):
    return jax.nn.sigmoid(v)


def _silu(v):
    return v * jax.nn.sigmoid(v)


def _dot(a, b):
    return jnp.dot(a, b, preferred_element_type=F32)


def _dot_nt(a, b):
    return lax.dot_general(a, b, (((1,), (1,)), ((), ())), preferred_element_type=F32)


def _params(sem):
    return pltpu.CompilerParams(dimension_semantics=sem, vmem_limit_bytes=VMEM_LIMIT)


def _rope_body(pos_ref, invf_ref, c_ref, s_ref):
    ang = pos_ref[...] * invf_ref[...]
    lane = lax.broadcasted_iota(jnp.int32, ang.shape, 1)
    rope = (lane >= MLA_NOPE) & (lane < MLA_NOPE + MLA_ROPE)
    c_ref[...] = jnp.where(lane < MLA_NOPE, 1.0, jnp.where(rope, jnp.cos(ang), 0.0))
    s_ref[...] = jnp.where(rope, jnp.sin(ang), 0.0)


def _rope_tables(pos_b, invf, tm):
    t = pos_b.shape[0]
    spec = pl.BlockSpec((tm, LANES), lambda i: (i, 0))
    return pl.pallas_call(
        _rope_body,
        grid=(t // tm,),
        in_specs=[spec, pl.BlockSpec((1, LANES), lambda i: (0, 0))],
        out_specs=[spec, spec],
        out_shape=[jax.ShapeDtypeStruct((t, LANES), F32)] * 2,
        compiler_params=_params(("parallel",)),
        name="rope_tables",
    )(pos_b, invf)


def _memkv_body(mem_ref, w_ref, o_ref):
    o_ref[0] = _dot(mem_ref[...], w_ref[0]).astype(BF16)


def _memkv(mem_b, w_b):
    depth = w_b.shape[0]
    rows = mem_b.shape[0]
    return pl.pallas_call(
        _memkv_body,
        grid=(depth,),
        in_specs=[pl.BlockSpec((rows, D_MODEL), lambda l: (0, 0)),
                  pl.BlockSpec((1, D_MODEL, 2 * X_WIDTH), lambda l: (l, 0, 0))],
        out_specs=pl.BlockSpec((1, rows, 2 * X_WIDTH), lambda l: (l, 0, 0)),
        out_shape=jax.ShapeDtypeStruct((depth, rows, 2 * X_WIDTH), BF16),
        compiler_params=_params(("parallel",)),
        name="mem_kv",
    )(mem_b, w_b)


_C_U = 0
_C_ZS = _C_U + SSM_WIDTH
_C_CQ = _C_ZS + SSM_WIDTH
_C_CKV = _C_CQ + MLA_Q_RANK
_C_KR = _C_CKV + MLA_KV_RANK
_C_ZM = _C_KR + 2 * LANES
_C_QX = _C_ZM + MLA_WIDTH
_C_ZX = _C_QX + X_WIDTH
_C_END = _C_ZX + X_WIDTH
_QW = MLA_HEADS * LANES


def _rms(v, g):
    return v * lax.rsqrt(jnp.mean(jnp.square(v), axis=-1, keepdims=True) + NORM_EPS) * g


def _inproj_body(x_ref, w1_ref, wq_ref, wkv_ref, qn_ref, kvn_ref, c_ref, s_ref,
                 u_ref, szs_ref, q_ref, k_ref, v_ref, szm_ref, qx_ref, szx_ref, u_scr):
    xb = x_ref[...].astype(BF16)

    def proj(c0, c1):
        return _dot(xb, w1_ref[0, :, c0:c1])

    u = proj(_C_U, _C_ZS)
    n_c = u_scr.shape[1] // SSM_CHUNK
    gpt = LANES // SSM_GROUP
    for j in range(SSM_WIDTH // LANES):
        u_scr[j] = u[:, j * LANES:(j + 1) * LANES]
    for s in range(SSM_CHUNK):
        for j in range(SSM_WIDTH // LANES):
            a = u_scr[j, pl.ds(s, n_c, stride=SSM_CHUNK), :].astype(BF16)
            for g in range(gpt):
                u_ref[j * gpt + g, :, s * SSM_GROUP:(s + 1) * SSM_GROUP] = a[:, g * SSM_GROUP:(g + 1) * SSM_GROUP]
    szs_ref[...] = _silu(proj(_C_ZS, _C_CQ)).astype(BF16)
    szm_ref[...] = _silu(proj(_C_ZM, _C_QX)).astype(BF16)
    qx_ref[...] = proj(_C_QX, _C_ZX).astype(BF16)
    szx_ref[...] = _silu(proj(_C_ZX, _C_END)).astype(BF16)

    cos = c_ref[...]
    sin = s_ref[...]
    scale = (MLA_NOPE + MLA_ROPE) ** -0.5 * math.log2(math.e)
    cos_q = cos * scale
    sin_q = sin * scale

    cq = _rms(proj(_C_CQ, _C_CKV), qn_ref[0]).astype(BF16)
    qa = _dot(cq, wq_ref[0])
    for h in range(MLA_HEADS):
        a = qa[:, h * LANES:(h + 1) * LANES]
        b = qa[:, _QW + h * LANES:_QW + (h + 1) * LANES]
        q_ref[:, h * LANES:(h + 1) * LANES] = (a * cos_q + b * sin_q).astype(BF16)

    ckv = _rms(proj(_C_CKV, _C_KR), kvn_ref[0]).astype(BF16)
    kv = _dot(ckv, wkv_ref[0])
    kr = proj(_C_KR, _C_ZM)
    kr = kr[:, :LANES] * cos + kr[:, LANES:] * sin
    lane = lax.broadcasted_iota(jnp.int32, (1, LANES), 1)
    one_hot = jnp.where(lane == MLA_V, 1.0, 0.0)
    for h in range(MLA_HEADS):
        k_ref[:, h * LANES:(h + 1) * LANES] = (kv[:, h * LANES:(h + 1) * LANES] + kr).astype(BF16)
        v_ref[:, h * LANES:(h + 1) * LANES] = (kv[:, _QW + h * LANES:_QW + (h + 1) * LANES] + one_hot).astype(BF16)


def _inproj(x, w1, wq, wkv, qn, kvn, cos, sin, layer, tm):
    t = x.shape[0]
    n_c = tm // SSM_CHUNK
    row = lambda w: pl.BlockSpec((tm, w), lambda i: (i, 0))
    lsel = lambda a: pl.BlockSpec((1,) + a.shape[1:], lambda i: (layer,) + (0,) * (a.ndim - 1))
    widths = (SSM_WIDTH, _QW, _QW, _QW, MLA_WIDTH, X_WIDTH, X_WIDTH)
    slab = (SSM_GROUPS, n_c, SSM_LC)
    return pl.pallas_call(
        _inproj_body,
        grid=(t // tm,),
        in_specs=[row(D_MODEL), lsel(w1), lsel(wq), lsel(wkv), lsel(qn), lsel(kvn), row(LANES), row(LANES)],
        out_specs=[pl.BlockSpec(slab, lambda i: (0, i, 0))] + [row(w) for w in widths],
        out_shape=[jax.ShapeDtypeStruct((SSM_GROUPS, t // SSM_CHUNK, SSM_LC), BF16)]
        + [jax.ShapeDtypeStruct((t, w), BF16) for w in widths],
        scratch_shapes=[pltpu.VMEM((SSM_WIDTH // LANES, tm, LANES), F32)],
        compiler_params=_params(("parallel",)),
        name="in_proj",
    )(x, w1, wq, wkv, qn, kvn, cos, sin)


def _ssm_prep_body(ldt_ref, a2_ref, bt2_ref, ct2_ref, toep_ref, bs_ref, cs_ref, al_ref):
    chunk = SSM_CHUNK
    dt = jnp.exp(ldt_ref[0])
    lr2 = a2_ref[0, 0:1, :]
    li2 = a2_ref[0, 1:2, :]
    lane2 = lax.broadcasted_iota(jnp.int32, (1, 2 * SSM_STATE), 1)
    first = lane2 < SSM_STATE

    def power(e):
        m = jnp.exp(lr2 * dt * e)
        return m * jnp.cos(li2 * dt * e), m * jnp.sin(li2 * dt * e)

    def per_row(v, reps):
        return jnp.broadcast_to(v[:, None, :], (v.shape[0], reps, v.shape[1])).reshape(v.shape[0] * reps, v.shape[1])

    lb_re, lb_im = power(1.0)
    nr, ni = lb_re - 1.0, lb_im
    den = lr2 * lr2 + li2 * li2
    f_re = (nr * lr2 + ni * li2) / den
    f_im = (ni * lr2 - nr * li2) / den
    bt_re = bt2_ref[0, 0]
    bt_im = bt2_ref[0, 1]
    bb_re = f_re * bt_re - f_im * bt_im
    bb_im = f_re * bt_im + f_im * bt_re

    kcol = lax.broadcasted_iota(jnp.int32, (chunk, 1), 0).astype(F32)
    pk_re, pk_im = power(kcol)
    pe_re, pe_im = power(chunk - 1.0 - kcol)

    pr, pi = per_row(pe_re, SSM_GROUP), per_row(pe_im, SSM_GROUP)
    br = jnp.tile(bb_re, (chunk, 1))
    bi = jnp.tile(bb_im, (chunk, 1))
    s_re = pr * br - pi * bi
    s_im = pr * bi + pi * br
    bs_ref[0, :, :2 * SSM_STATE] = jnp.where(first, s_re, s_im).astype(BF16)
    bs_ref[0, :, 2 * SSM_STATE:] = jnp.where(first, s_im, s_re).astype(BF16)

    qr, qi = per_row(pk_re, SSM_GROUP), per_row(pk_im, SSM_GROUP)
    c_re = jnp.tile(ct2_ref[0, 0], (chunk, 1))
    c_im = jnp.tile(ct2_ref[0, 1], (chunk, 1))
    m_re = c_re * qr - c_im * qi
    m_im = c_re * qi + c_im * qr
    hi = lax.Precision.HIGHEST
    nt = (((1,), (1,)), ((), ()))
    kcat = (lax.dot_general(bb_re, jnp.where(first, m_re, 0.0), nt, precision=hi, preferred_element_type=F32)
            - lax.dot_general(bb_im, jnp.where(first, m_im, 0.0), nt, precision=hi, preferred_element_type=F32))
    lane = lax.broadcasted_iota(jnp.int32, (SSM_GROUP, SSM_LC), 1)
    for s in range(chunk):
        shifted = kcat if s == 0 else pltpu.roll(kcat, SSM_GROUP * s, 1)
        toep_ref[0, s * SSM_GROUP:(s + 1) * SSM_GROUP, :] = jnp.where(
            lane >= SSM_GROUP * s, shifted, 0.0).astype(BF16)

    n_re = m_re * lb_re - m_im * lb_im
    n_im = m_re * lb_im + m_im * lb_re
    cs_ref[0] = jnp.where(first, n_re, -n_im).T.astype(BF16)

    a_re, a_im = power(float(chunk))
    al_ref[0, 0:1, :] = a_re
    al_ref[0, 1:2, :] = jnp.where(first, -a_im, a_im)


def _ssm_prep(ldt, a2, bt2, ct2):
    n = ldt.shape[0]
    sel = lambda a: pl.BlockSpec((1,) + a.shape[1:], lambda i: (i,) + (0,) * (a.ndim - 1))
    out_shapes = [jax.ShapeDtypeStruct((n, SSM_LC, SSM_LC), BF16),
                  jax.ShapeDtypeStruct((n, SSM_LC, 4 * SSM_STATE), BF16),
                  jax.ShapeDtypeStruct((n, 2 * SSM_STATE, SSM_LC), BF16),
                  jax.ShapeDtypeStruct((n, 2, 2 * SSM_STATE), F32)]
    return pl.pallas_call(
        _ssm_prep_body,
        grid=(n,),
        in_specs=[sel(ldt), sel(a2), sel(bt2), sel(ct2)],
        out_specs=[sel(s) for s in out_shapes],
        out_shape=out_shapes,
        compiler_params=_params(("parallel",)),
        name="ssm_prep",
    )(ldt, a2, bt2, ct2)


def _ssm_body(u_ref, toep_ref, bs_ref, cs_ref, al_ref, d_ref, o_ref, hloc_ref, hst_ref, *, batch, n_chunks):
    gb = u_ref.shape[0]
    w = 2 * SSM_STATE
    for j in range(gb):
        hl = _dot(u_ref[j], bs_ref[j])
        hloc_ref[j, 0] = hl[:, :w]
        hloc_ref[j, 1] = hl[:, w:]

    a1 = [jnp.broadcast_to(al_ref[j, 0:1, :], (batch, 2 * SSM_STATE)) for j in range(gb)]
    a2 = [jnp.broadcast_to(al_ref[j, 1:2, :], (batch, 2 * SSM_STATE)) for j in range(gb)]

    def step(c, carry):
        out = []
        rows = pl.ds(c, batch, stride=n_chunks)
        for j in range(gb):
            h, hs = carry[j]
            hst_ref[j, rows, :] = h
            out.append((a1[j] * h + a2[j] * hs + hloc_ref[j, 0, rows, :],
                        a1[j] * hs - a2[j] * h + hloc_ref[j, 1, rows, :]))
        return tuple(out)

    zero = jnp.zeros((batch, 2 * SSM_STATE), F32)
    lax.fori_loop(0, n_chunks, step, tuple((zero, zero) for _ in range(gb)), unroll=4)

    for j in range(gb):
        u = u_ref[j]
        y = (_dot(u, toep_ref[j]) + _dot(hst_ref[j].astype(BF16), cs_ref[j])
             + d_ref[j] * u.astype(F32))
        o_ref[j] = jax.nn.gelu(y).astype(BF16)


def _ssm(u_g, toep, bs, cs, al, d_e, layer, batch):
    groups, rows, _ = u_g.shape
    n_chunks = rows // batch
    gb = SSM_GB
    lsel = lambda a: pl.BlockSpec((gb,) + a.shape[1:],
                                  lambda i: (layer * (groups // gb) + i,) + (0,) * (a.ndim - 1))
    useg = pl.BlockSpec((gb, rows, SSM_LC), lambda i: (i, 0, 0))
    return pl.pallas_call(
        functools.partial(_ssm_body, batch=batch, n_chunks=n_chunks),
        grid=(groups // gb,),
        in_specs=[useg, lsel(toep), lsel(bs), lsel(cs), lsel(al), lsel(d_e)],
        out_specs=useg,
        out_shape=jax.ShapeDtypeStruct(u_g.shape, BF16),
        scratch_shapes=[pltpu.VMEM((gb, 2, rows, 2 * SSM_STATE), F32),
                        pltpu.VMEM((gb, rows, 2 * SSM_STATE), F32)],
        compiler_params=_params(("parallel",)),
        name="s5_scan",
    )(u_g, toep, bs, cs, al, d_e)


def _mla_body(q_ref, k_ref, v_ref, szm_ref, o_ref, sa_ref, sb_ref, m_ref, acc_ref, *, tq):
    qi = pl.program_id(2)
    row = lax.broadcasted_iota(jnp.int32, (tq, tq), 0)
    col = lax.broadcasted_iota(jnp.int32, (tq, tq), 1)
    causal = row >= col
    heads = [slice(h * LANES, (h + 1) * LANES) for h in range(MLA_HB)]

    def scores(ki, dst_ref):
        k0 = pl.multiple_of(ki * tq, tq)
        for h, hl in enumerate(heads):
            dst_ref[h] = _dot_nt(q_ref[0, :, hl], k_ref[0, pl.ds(k0, tq), hl])

    def consume(ki, src_ref, masked):
        k0 = pl.multiple_of(ki * tq, tq)
        for h, hl in enumerate(heads):
            s = src_ref[h]
            if masked:
                s = jnp.where(causal, s, -1e30)
            m = m_ref[h]
            m_new = jnp.maximum(m, jnp.max(s, axis=1, keepdims=True))
            alpha = jnp.exp2(m - m_new)
            p = jnp.exp2((s - m_new).astype(BF16))
            acc_ref[h] = alpha * acc_ref[h] + _dot(p, v_ref[0, pl.ds(k0, tq), hl])
            m_ref[h] = m_new

    m_ref[...] = jnp.full(m_ref.shape, -1e30, F32)
    acc_ref[...] = jnp.zeros(acc_ref.shape, F32)
    scores(0, sa_ref)

    def pair(j, carry):
        scores(2 * j + 1, sb_ref)
        consume(2 * j, sa_ref, False)
        scores(2 * j + 2, sa_ref)
        consume(2 * j + 1, sb_ref, False)
        return carry

    lax.fori_loop(0, qi // 2, pair, 0)

    @pl.when(qi % 2 == 0)
    def _():
        consume(qi, sa_ref, True)

    @pl.when(qi % 2 == 1)
    def _():
        scores(qi, sb_ref)
        consume(qi - 1, sa_ref, False)
        consume(qi, sb_ref, True)

    lane = lax.broadcasted_iota(jnp.int32, (tq, LANES), 1)
    for hp in range(MLA_HB // 2):
        o0 = acc_ref[2 * hp]
        o1 = acc_ref[2 * hp + 1]
        o0 = o0 / o0[:, MLA_V:MLA_V + 1]
        o1 = o1 / o1[:, MLA_V:MLA_V + 1]
        o = jnp.where(lane < MLA_V, o0, pltpu.roll(o1, MLA_V, 1))
        sl = slice(hp * LANES, (hp + 1) * LANES)
        o_ref[0, :, sl] = (o * szm_ref[0, :, sl].astype(F32)).astype(BF16)


def _mla(q, k, v, szm, tq):
    b, s, _ = q.shape
    hb = MLA_HB
    qkv = lambda rows, blk: pl.BlockSpec((1, rows, hb * LANES), blk)
    outw = hb * MLA_V
    return pl.pallas_call(
        functools.partial(_mla_body, tq=tq),
        grid=(b, MLA_HEADS // hb, s // tq),
        in_specs=[qkv(tq, lambda bi, hp, qi: (bi, qi, hp)),
                  qkv(s, lambda bi, hp, qi: (bi, 0, hp)),
                  qkv(s, lambda bi, hp, qi: (bi, 0, hp)),
                  pl.BlockSpec((1, tq, outw), lambda bi, hp, qi: (bi, qi, hp))],
        out_specs=pl.BlockSpec((1, tq, outw), lambda bi, hp, qi: (bi, qi, hp)),
        out_shape=jax.ShapeDtypeStruct((b, s, MLA_WIDTH), BF16),
        scratch_shapes=[pltpu.VMEM((hb, tq, tq), F32), pltpu.VMEM((hb, tq, tq), F32),
                        pltpu.VMEM((hb, tq, 1), F32), pltpu.VMEM((hb, tq, LANES), F32)],
        compiler_params=_params(("parallel", "parallel", "arbitrary")),
        name="mla_attention",
    )(q, k, v, szm)


def _memattn_body(q_ref, kv_ref, szx_ref, o_ref):
    scale = X_HEAD_DIM ** -0.5
    for h in range(X_HEADS):
        sl = slice(h * X_HEAD_DIM, (h + 1) * X_HEAD_DIM)
        q = q_ref[0, :, sl]
        k = kv_ref[0, 0, :, sl]
        v = kv_ref[0, 0, :, X_WIDTH + h * X_HEAD_DIM:X_WIDTH + (h + 1) * X_HEAD_DIM]
        s = _dot_nt(q, k) * scale
        p = jnp.exp(s - jnp.max(s, axis=1, keepdims=True))
        o = _dot(p.astype(BF16), v) / jnp.sum(p, axis=1, keepdims=True)
        o_ref[0, :, sl] = (o * szx_ref[0, :, sl].astype(F32)).astype(BF16)


def _memattn(qx, kvm, szx, layer, tm):
    b, s, _ = qx.shape
    m = kvm.shape[2]
    tile = pl.BlockSpec((1, tm, X_WIDTH), lambda bi, i: (bi, i, 0))
    return pl.pallas_call(
        _memattn_body,
        grid=(b, s // tm),
        in_specs=[tile, pl.BlockSpec((1, 1, m, 2 * X_WIDTH), lambda bi, i: (layer, bi, 0, 0)), tile],
        out_specs=tile,
        out_shape=jax.ShapeDtypeStruct((b, s, X_WIDTH), BF16),
        compiler_params=_params(("parallel", "parallel")),
        name="mem_attention",
    )(qx, kvm, szx)


def _merge_body(x_ref, g0_ref, gn_ref, szs_ref, ymla_ref, ymem_ref, wg_ref, bg_ref, wglu_ref, bglu_ref,
                pssm_ref, pmla_ref, pmem_ref, wout_ref, lng_ref, lnb_ref, o_ref, st_scr, gt_scr, *, alpha):
    tiles = SSM_WIDTH // LANES

    def to_token_major(g_ref):
        n_c = g_ref.shape[1]
        for g in range(SSM_GROUPS):
            blk = g_ref[g].astype(F32)
            for s in range(SSM_CHUNK):
                st_scr[s, :, g * SSM_GROUP:(g + 1) * SSM_GROUP] = blk[:, s * SSM_GROUP:(s + 1) * SSM_GROUP]
        for s in range(SSM_CHUNK):
            for j in range(tiles):
                gt_scr[j, pl.ds(s, n_c, stride=SSM_CHUNK), :] = st_scr[s, :, j * LANES:(j + 1) * LANES]

    @pl.when(pl.program_id(0) == 0)
    def _():
        to_token_major(g0_ref)

    x = x_ref[...]
    xb = x.astype(BF16)
    gt = jnp.concatenate([gt_scr[j] for j in range(tiles)], axis=1).astype(BF16)
    to_token_major(gn_ref)
    t = _dot(gt, wglu_ref[0]) + bglu_ref[0]
    y_ssm = (t[:, :SSM_WIDTH] * _sigmoid(t[:, SSM_WIDTH:]) * szs_ref[...].astype(F32)).astype(BF16)
    merged = None
    for i, (y, p_ref) in enumerate(((y_ssm, pssm_ref), (ymla_ref[...], pmla_ref), (ymem_ref[...], pmem_ref))):
        cols = slice(i * D_MODEL, (i + 1) * D_MODEL)
        gate = _sigmoid(_dot(xb, wg_ref[0, :, cols]) + bg_ref[0, :, cols])
        term = gate * _dot(y, p_ref[0])
        merged = term if merged is None else merged + term
    r = alpha * x + _dot(merged.astype(BF16), wout_ref[0])
    mu = jnp.mean(r, axis=-1, keepdims=True)
    var = jnp.mean(jnp.square(r - mu), axis=-1, keepdims=True)
    o_ref[...] = (r - mu) * lax.rsqrt(var + NORM_EPS) * lng_ref[0] + lnb_ref[0]


def _merge(x, g, szs, ymla, ymem, wg, bg, wglu, bglu, pssm, pmla, pmem, wout, lng, lnb, layer, tm, alpha):
    t = x.shape[0]
    row = lambda w: pl.BlockSpec((tm, w), lambda i: (i, 0))
    lsel = lambda a: pl.BlockSpec((1,) + a.shape[1:], lambda i: (layer,) + (0,) * (a.ndim - 1))
    weights = (wg, bg, wglu, bglu, pssm, pmla, pmem, wout, lng, lnb)
    n_c = tm // SSM_CHUNK
    last = t // tm - 1
    slab = (SSM_GROUPS, n_c, SSM_LC)
    return pl.pallas_call(
        functools.partial(_merge_body, alpha=alpha),
        grid=(t // tm,),
        in_specs=[row(D_MODEL),
                  pl.BlockSpec(slab, lambda i: (0, 0, 0)),
                  pl.BlockSpec(slab, lambda i: (0, jnp.minimum(i + 1, last), 0)),
                  row(SSM_WIDTH), row(MLA_WIDTH), row(X_WIDTH)]
        + [lsel(w) for w in weights],
        out_specs=row(D_MODEL),
        out_shape=jax.ShapeDtypeStruct((t, D_MODEL), F32),
        scratch_shapes=[pltpu.VMEM((SSM_CHUNK, n_c, SSM_WIDTH), F32),
                        pltpu.VMEM((SSM_WIDTH // LANES, tm, LANES), F32)],
        compiler_params=_params(("arbitrary",)),
        name="merge_out_ln",
    )(x, g, g, szs, ymla, ymem, *weights)


def _rotate_half_cols(w):
    half = w.shape[-1] // 2
    return jnp.concatenate([-w[..., half:], w[..., :half]], axis=-1)


def _pack_weights(w_in, w_uq, w_ukv):
    depth = w_in.shape[0]
    split = (SSM_WIDTH, SSM_WIDTH, MLA_Q_RANK, MLA_KV_RANK, MLA_ROPE, MLA_WIDTH, X_WIDTH, X_WIDTH)
    offs = [0]
    for w in split:
        offs.append(offs[-1] + w)
    seg = [w_in[:, :, offs[i]:offs[i + 1]] for i in range(len(split))]
    w_u, w_zs, w_cq, w_ckv, w_kr, w_zm, w_qx, w_zx = seg
    w_gate = w_in[:, :, offs[-1]:]

    def rope_tile(w):
        z = jnp.zeros(w.shape[:-1] + (MLA_NOPE,), w.dtype)
        z2 = jnp.zeros(w.shape[:-1] + (LANES - MLA_NOPE - MLA_ROPE,), w.dtype)
        return jnp.concatenate([z, w, z2], axis=-1)

    w1 = jnp.concatenate([w_u, w_zs, w_cq, w_ckv, rope_tile(w_kr), rope_tile(_rotate_half_cols(w_kr)),
                          w_zm, w_qx, w_zx], axis=-1).astype(BF16)

    uq = w_uq.reshape(depth, MLA_Q_RANK, MLA_HEADS, MLA_NOPE + MLA_ROPE)
    q_nope, q_rope = uq[..., :MLA_NOPE], uq[..., MLA_NOPE:]
    pad = jnp.zeros(q_rope.shape, uq.dtype)
    q_main = jnp.concatenate([q_nope, q_rope, pad], axis=-1)
    q_rot = jnp.concatenate([jnp.zeros(q_nope.shape, uq.dtype), _rotate_half_cols(q_rope), pad], axis=-1)
    wq = jnp.concatenate([q_main.reshape(depth, MLA_Q_RANK, _QW), q_rot.reshape(depth, MLA_Q_RANK, _QW)],
                         axis=-1).astype(BF16)

    ukv = w_ukv.reshape(depth, MLA_KV_RANK, MLA_HEADS, MLA_NOPE + MLA_V)
    k_nope, v = ukv[..., :MLA_NOPE], ukv[..., MLA_NOPE:]
    k_tile = jnp.concatenate([k_nope, jnp.zeros(k_nope.shape, ukv.dtype)], axis=-1)
    v_tile = jnp.concatenate([v, jnp.zeros(v.shape, ukv.dtype)], axis=-1)
    wkv = jnp.concatenate([k_tile.reshape(depth, MLA_KV_RANK, _QW), v_tile.reshape(depth, MLA_KV_RANK, _QW)],
                          axis=-1).astype(BF16)
    return w1, w_gate.astype(BF16), wq, wkv


def _pack_ssm_params(a_re, a_im, log_dt, b_re, b_im, c_re, c_im, d):
    depth = a_re.shape[0]
    n = depth * SSM_GROUPS
    ar = a_re.reshape(n, SSM_STATE)
    ai = a_im.reshape(n, SSM_STATE)
    ldt = jnp.broadcast_to(log_dt.reshape(n, 1, 1), (n, 1, 2 * SSM_STATE))
    a2 = jnp.stack([jnp.concatenate([ar, ar], -1), jnp.concatenate([ai, ai], -1)], axis=1)
    bt = jnp.stack([b_re.reshape(n, SSM_STATE, SSM_GROUP), b_im.reshape(n, SSM_STATE, SSM_GROUP)], axis=1)
    bt = jnp.swapaxes(bt, -1, -2)
    bt2 = jnp.concatenate([bt, bt], axis=-1)
    ct = jnp.stack([c_re.reshape(n, SSM_GROUP, SSM_STATE), c_im.reshape(n, SSM_GROUP, SSM_STATE)], axis=1)
    ct2 = jnp.concatenate([ct, ct], axis=-1)
    d_e = jnp.tile(d.reshape(n, 1, SSM_GROUP), (1, 1, SSM_CHUNK))
    return ldt, a2, bt2, ct2, d_e


def kernel(x, mem, positions, w_in, b_gate, ssm_a_re, ssm_a_im, ssm_log_dt, ssm_b_re, ssm_b_im, ssm_c_re,
           ssm_c_im, ssm_d, w_glu, b_glu, mla_q_norm, w_uq, mla_kv_norm, w_ukv, w_mem_kv, p_ssm, p_mla, p_mem,
           w_out, ln_g, ln_b):
    bsz, s, d = x.shape
    depth = w_in.shape[0]
    mlen = mem.shape[1]
    t = bsz * s
    assert d == D_MODEL and s % SSM_CHUNK == 0 and SSM_GROUPS % SSM_GB == 0
    tm = min(512, s)
    tq = min(512, s)
    assert s % tm == 0 and s % tq == 0
    alpha = (2 * depth) ** 0.25
    n_chunks = s // SSM_CHUNK

    w1, w_gate, wq, wkv = _pack_weights(w_in, w_uq, w_ukv)
    row3 = lambda a: a.reshape(depth, 1, a.shape[-1])
    qn, kvn = row3(mla_q_norm), row3(mla_kv_norm)
    bg, bglu, lng, lnb = row3(b_gate), row3(b_glu), row3(ln_g), row3(ln_b)
    wglu, pssm, pmla, pmem, wout = (a.astype(BF16) for a in (w_glu, p_ssm, p_mla, p_mem, w_out))

    inv_freq = ROPE_THETA ** (-jnp.arange(0, MLA_ROPE, 2, dtype=F32) / MLA_ROPE)
    invf = jnp.concatenate([jnp.zeros((MLA_NOPE,), F32), inv_freq, inv_freq,
                            jnp.zeros((LANES - MLA_NOPE - MLA_ROPE,), F32)]).reshape(1, LANES)
    pos_b = jnp.broadcast_to(positions.astype(F32).reshape(t, 1), (t, LANES))
    cos, sin = _rope_tables(pos_b, invf, tm)

    kvm = _memkv(mem.reshape(bsz * mlen, d).astype(BF16), w_mem_kv.astype(BF16))
    kvm = kvm.reshape(depth, bsz, mlen, 2 * X_WIDTH)

    ldt, a2, bt2, ct2, d_e = _pack_ssm_params(ssm_a_re, ssm_a_im, ssm_log_dt, ssm_b_re, ssm_b_im,
                                              ssm_c_re, ssm_c_im, ssm_d)
    toep, bs, cs, al = _ssm_prep(ldt, a2, bt2, ct2)

    xf = x.reshape(t, d)
    for layer in range(depth):
        u_g, szs, q, k, v, szm, qx, szx = _inproj(xf, w1, wq, wkv, qn, kvn, cos, sin, layer, tm)
        g = _ssm(u_g, toep, bs, cs, al, d_e, layer, bsz)
        y_mla = _mla(q.reshape(bsz, s, _QW), k.reshape(bsz, s, _QW), v.reshape(bsz, s, _QW),
                     szm.reshape(bsz, s, MLA_WIDTH), tq)
        y_mem = _memattn(qx.reshape(bsz, s, X_WIDTH), kvm, szx.reshape(bsz, s, X_WIDTH), layer, tm)

        xf = _merge(xf, g, szs, y_mla.reshape(t, MLA_WIDTH), y_mem.reshape(t, X_WIDTH), w_gate, bg, wglu, bglu,
                    pssm, pmla, pmem, wout, lng, lnb, layer, tm, alpha)
    return xf.reshape(bsz, s, d)
```

```python
import functools
import math

import jax
import jax.numpy as jnp
from jax import lax
from jax.experimental import pallas as pl
from jax.experimental.pallas import tpu as pltpu

F32 = jnp.float32
BF16 = jnp.bfloat16

LANES = 128
D_MODEL = 1024
SSM_WIDTH = 512
SSM_GROUP = 16
SSM_GROUPS = SSM_WIDTH // SSM_GROUP
SSM_STATE = 64
MLA_HEADS = 8
MLA_NOPE = 64
MLA_ROPE = 32
MLA_V = 64
MLA_Q_RANK = 256
MLA_KV_RANK = 128
MLA_WIDTH = MLA_HEADS * MLA_V
ROPE_THETA = 10000.0
X_HEADS = 4
X_HEAD_DIM = 128
X_WIDTH = X_HEADS * X_HEAD_DIM
N_BRANCH = 3
NORM_EPS = 1e-5

SSM_CHUNK = 32
SSM_LC = SSM_CHUNK * SSM_GROUP
SSM_GB = 4
MLA_HB = 4
VMEM_LIMIT = 48 * 1024 * 1024


def _sigmoid(v):
    return jax.nn.sigmoid(v)


def _silu(v):
    return v * jax.nn.sigmoid(v)


def _dot(a, b):
    return jnp.dot(a, b, preferred_element_type=F32)


def _dot_nt(a, b):
    return lax.dot_general(a, b, (((1,), (1,)), ((), ())), preferred_element_type=F32)


def _params(sem):
    return pltpu.CompilerParams(dimension_semantics=sem, vmem_limit_bytes=VMEM_LIMIT)


def _rope_body(pos_ref, invf_ref, c_ref, s_ref):
    ang = pos_ref[...] * invf_ref[...]
    lane = lax.broadcasted_iota(jnp.int32, ang.shape, 1)
    rope = (lane >= MLA_NOPE) & (lane < MLA_NOPE + MLA_ROPE)
    c_ref[...] = jnp.where(lane < MLA_NOPE, 1.0, jnp.where(rope, jnp.cos(ang), 0.0))
    s_ref[...] = jnp.where(rope, jnp.sin(ang), 0.0)


def _rope_tables(pos_b, invf, tm):
    t = pos_b.shape[0]
    spec = pl.BlockSpec((tm, LANES), lambda i: (i, 0))
    return pl.pallas_call(
        _rope_body,
        grid=(t // tm,),
        in_specs=[spec, pl.BlockSpec((1, LANES), lambda i: (0, 0))],
        out_specs=[spec, spec],
        out_shape=[jax.ShapeDtypeStruct((t, LANES), F32)] * 2,
        compiler_params=_params(("parallel",)),
        name="rope_tables",
    )(pos_b, invf)


def _memkv_body(mem_ref, w_ref, o_ref):
    o_ref[0] = _dot(mem_ref[...], w_ref[0]).astype(BF16)


def _memkv(mem_b, w_b):
    depth = w_b.shape[0]
    rows = mem_b.shape[0]
    return pl.pallas_call(
        _memkv_body,
        grid=(depth,),
        in_specs=[pl.BlockSpec((rows, D_MODEL), lambda l: (0, 0)),
                  pl.BlockSpec((1, D_MODEL, 2 * X_WIDTH), lambda l: (l, 0, 0))],
        out_specs=pl.BlockSpec((1, rows, 2 * X_WIDTH), lambda l: (l, 0, 0)),
        out_shape=jax.ShapeDtypeStruct((depth, rows, 2 * X_WIDTH), BF16),
        compiler_params=_params(("parallel",)),
        name="mem_kv",
    )(mem_b, w_b)


_C_U = 0
_C_ZS = _C_U + SSM_WIDTH
_C_CQ = _C_ZS + SSM_WIDTH
_C_CKV = _C_CQ + MLA_Q_RANK
_C_KR = _C_CKV + MLA_KV_RANK
_C_ZM = _C_KR + 2 * LANES
_C_QX = _C_ZM + MLA_WIDTH
_C_ZX = _C_QX + X_WIDTH
_C_END = _C_ZX + X_WIDTH
_QW = MLA_HEADS * LANES


def _rms(v, g):
    return v * lax.rsqrt(jnp.mean(jnp.square(v), axis=-1, keepdims=True) + NORM_EPS) * g


def _inproj_body(x_ref, w1_ref, wq_ref, wk_ref, wvt_ref, qn_ref, kvn_ref, c_ref, s_ref,
                 u_ref, vt_ref, szs_ref, q_ref, k_ref, szm_ref, qx_ref, szx_ref, u_scr):
    xb = x_ref[...].astype(BF16)

    def proj(c0, c1):
        return _dot(xb, w1_ref[0, :, c0:c1])

    u = proj(_C_U, _C_ZS)
    n_c = u_scr.shape[1] // SSM_CHUNK
    gpt = LANES // SSM_GROUP
    for j in range(SSM_WIDTH // LANES):
        u_scr[j] = u[:, j * LANES:(j + 1) * LANES]
    for s in range(SSM_CHUNK):
        for j in range(SSM_WIDTH // LANES):
            a = u_scr[j, pl.ds(s, n_c, stride=SSM_CHUNK), :].astype(BF16)
            for g in range(gpt):
                u_ref[j * gpt + g, :, s * SSM_GROUP:(s + 1) * SSM_GROUP] = a[:, g * SSM_GROUP:(g + 1) * SSM_GROUP]
    szs_ref[...] = _silu(proj(_C_ZS, _C_CQ)).astype(BF16)
    szm_ref[...] = _silu(proj(_C_ZM, _C_QX)).astype(BF16)
    qx_ref[...] = proj(_C_QX, _C_ZX).astype(BF16)
    szx_ref[...] = _silu(proj(_C_ZX, _C_END)).astype(BF16)

    cos = c_ref[...]
    sin = s_ref[...]
    scale = (MLA_NOPE + MLA_ROPE) ** -0.5 * math.log2(math.e)
    cos_q = cos * scale
    sin_q = sin * scale

    cq = _rms(proj(_C_CQ, _C_CKV), qn_ref[0]).astype(BF16)
    qa = _dot(cq, wq_ref[0])
    for h in range(MLA_HEADS):
        a = qa[:, h * LANES:(h + 1) * LANES]
        b = qa[:, _QW + h * LANES:_QW + (h + 1) * LANES]
        q_ref[:, h * LANES:(h + 1) * LANES] = (a * cos_q + b * sin_q).astype(BF16)

    ckv = _rms(proj(_C_CKV, _C_KR), kvn_ref[0]).astype(BF16)
    kn = _dot(ckv, wk_ref[0])
    kr = proj(_C_KR, _C_ZM)
    kr = kr[:, :LANES] * cos + kr[:, LANES:] * sin
    for h in range(MLA_HEADS):
        k_ref[:, h * LANES:(h + 1) * LANES] = (kn[:, h * LANES:(h + 1) * LANES] + kr).astype(BF16)
    sub = lax.broadcasted_iota(jnp.int32, (_QW, 1), 0)
    ones_row = jnp.where(sub % LANES == MLA_V, 1.0, 0.0)
    vt_ref[0, 0] = (_dot_nt(wvt_ref[0], ckv) + ones_row).astype(BF16)


def _inproj(x, w1, wq, wk, wvt, qn, kvn, cos, sin, layer, tm, seq):
    t = x.shape[0]
    n_c = tm // SSM_CHUNK
    nblk = seq // tm
    row = lambda w: pl.BlockSpec((tm, w), lambda i: (i, 0))
    lsel = lambda a: pl.BlockSpec((1,) + a.shape[1:], lambda i: (layer,) + (0,) * (a.ndim - 1))
    widths = (SSM_WIDTH, _QW, _QW, MLA_WIDTH, X_WIDTH, X_WIDTH)
    slab = (SSM_GROUPS, n_c, SSM_LC)
    return pl.pallas_call(
        _inproj_body,
        grid=(t // tm,),
        in_specs=[row(D_MODEL), lsel(w1), lsel(wq), lsel(wk), lsel(wvt), lsel(qn), lsel(kvn),
                  row(LANES), row(LANES)],
        out_specs=[pl.BlockSpec(slab, lambda i: (0, i, 0)),
                   pl.BlockSpec((1, 1, _QW, tm), lambda i: (i // nblk, i % nblk, 0, 0))]
        + [row(w) for w in widths],
        out_shape=[jax.ShapeDtypeStruct((SSM_GROUPS, t // SSM_CHUNK, SSM_LC), BF16),
                   jax.ShapeDtypeStruct((t // seq, nblk, _QW, tm), BF16)]
        + [jax.ShapeDtypeStruct((t, w), BF16) for w in widths],
        scratch_shapes=[pltpu.VMEM((SSM_WIDTH // LANES, tm, LANES), F32)],
        compiler_params=_params(("parallel",)),
        name="in_proj",
    )(x, w1, wq, wk, wvt, qn, kvn, cos, sin)


def _ssm_prep_body(ldt_ref, a2_ref, bt2_ref, ct2_ref, toep_ref, bs_ref, cs_ref, al_ref):
    chunk = SSM_CHUNK
    dt = jnp.exp(ldt_ref[0])
    lr2 = a2_ref[0, 0:1, :]
    li2 = a2_ref[0, 1:2, :]
    lane2 = lax.broadcasted_iota(jnp.int32, (1, 2 * SSM_STATE), 1)
    first = lane2 < SSM_STATE

    def power(e):
        m = jnp.exp(lr2 * dt * e)
        return m * jnp.cos(li2 * dt * e), m * jnp.sin(li2 * dt * e)

    def per_row(v, reps):
        return jnp.broadcast_to(v[:, None, :], (v.shape[0], reps, v.shape[1])).reshape(v.shape[0] * reps, v.shape[1])

    lb_re, lb_im = power(1.0)
    nr, ni = lb_re - 1.0, lb_im
    den = lr2 * lr2 + li2 * li2
    f_re = (nr * lr2 + ni * li2) / den
    f_im = (ni * lr2 - nr * li2) / den
    bt_re = bt2_ref[0, 0]
    bt_im = bt2_ref[0, 1]
    bb_re = f_re * bt_re - f_im * bt_im
    bb_im = f_re * bt_im + f_im * bt_re

    kcol = lax.broadcasted_iota(jnp.int32, (chunk, 1), 0).astype(F32)
    pk_re, pk_im = power(kcol)
    pe_re, pe_im = power(chunk - 1.0 - kcol)

    pr, pi = per_row(pe_re, SSM_GROUP), per_row(pe_im, SSM_GROUP)
    br = jnp.tile(bb_re, (chunk, 1))
    bi = jnp.tile(bb_im, (chunk, 1))
    s_re = pr * br - pi * bi
    s_im = pr * bi + pi * br
    bs_ref[0, :, :2 * SSM_STATE] = jnp.where(first, s_re, s_im).astype(BF16)
    bs_ref[0, :, 2 * SSM_STATE:] = jnp.where(first, s_im, s_re).astype(BF16)

    qr, qi = per_row(pk_re, SSM_GROUP), per_row(pk_im, SSM_GROUP)
    c_re = jnp.tile(ct2_ref[0, 0], (chunk, 1))
    c_im = jnp.tile(ct2_ref[0, 1], (chunk, 1))
    m_re = c_re * qr - c_im * qi
    m_im = c_re * qi + c_im * qr
    hi = lax.Precision.HIGHEST
    nt = (((1,), (1,)), ((), ()))
    kcat = (lax.dot_general(bb_re, jnp.where(first, m_re, 0.0), nt, precision=hi, preferred_element_type=F32)
            - lax.dot_general(bb_im, jnp.where(first, m_im, 0.0), nt, precision=hi, preferred_element_type=F32))
    lane = lax.broadcasted_iota(jnp.int32, (SSM_GROUP, SSM_LC), 1)
    for s in range(chunk):
        shifted = kcat if s == 0 else pltpu.roll(kcat, SSM_GROUP * s, 1)
        toep_ref[0, s * SSM_GROUP:(s + 1) * SSM_GROUP, :] = jnp.where(
            lane >= SSM_GROUP * s, shifted, 0.0).astype(BF16)

    n_re = m_re * lb_re - m_im * lb_im
    n_im = m_re * lb_im + m_im * lb_re
    cs_ref[0] = jnp.where(first, n_re, -n_im).T.astype(BF16)

    a_re, a_im = power(float(chunk))
    al_ref[0, 0:1, :] = a_re
    al_ref[0, 1:2, :] = jnp.where(first, -a_im, a_im)


def _ssm_prep(ldt, a2, bt2, ct2):
    n = ldt.shape[0]
    sel = lambda a: pl.BlockSpec((1,) + a.shape[1:], lambda i: (i,) + (0,) * (a.ndim - 1))
    out_shapes = [jax.ShapeDtypeStruct((n, SSM_LC, SSM_LC), BF16),
                  jax.ShapeDtypeStruct((n, SSM_LC, 4 * SSM_STATE), BF16),
                  jax.ShapeDtypeStruct((n, 2 * SSM_STATE, SSM_LC), BF16),
                  jax.ShapeDtypeStruct((n, 2, 2 * SSM_STATE), F32)]
    return pl.pallas_call(
        _ssm_prep_body,
        grid=(n,),
        in_specs=[sel(ldt), sel(a2), sel(bt2), sel(ct2)],
        out_specs=[sel(s) for s in out_shapes],
        out_shape=out_shapes,
        compiler_params=_params(("parallel",)),
        name="ssm_prep",
    )(ldt, a2, bt2, ct2)


def _ssm_body(u_ref, toep_ref, bs_ref, cs_ref, al_ref, d_ref, o_ref, hloc_ref, hst_ref, *, batch, n_chunks):
    gb = u_ref.shape[0]
    w = 2 * SSM_STATE
    for j in range(gb):
        hl = _dot(u_ref[j], bs_ref[j])
        hloc_ref[j, 0] = hl[:, :w]
        hloc_ref[j, 1] = hl[:, w:]

    a1 = [jnp.broadcast_to(al_ref[j, 0:1, :], (batch, 2 * SSM_STATE)) for j in range(gb)]
    a2 = [jnp.broadcast_to(al_ref[j, 1:2, :], (batch, 2 * SSM_STATE)) for j in range(gb)]

    def step(c, carry):
        out = []
        rows = pl.ds(c, batch, stride=n_chunks)
        for j in range(gb):
            h, hs = carry[j]
            hst_ref[j, rows, :] = h
            out.append((a1[j] * h + a2[j] * hs + hloc_ref[j, 0, rows, :],
                        a1[j] * hs - a2[j] * h + hloc_ref[j, 1, rows, :]))
        return tuple(out)

    zero = jnp.zeros((batch, 2 * SSM_STATE), F32)
    lax.fori_loop(0, n_chunks, step, tuple((zero, zero) for _ in range(gb)), unroll=4)

    for j in range(gb):
        u = u_ref[j]
        y = (_dot(u, toep_ref[j]) + _dot(hst_ref[j].astype(BF16), cs_ref[j])
             + d_ref[j] * u.astype(F32))
        o_ref[j] = jax.nn.gelu(y).astype(BF16)


def _ssm(u_g, toep, bs, cs, al, d_e, layer, batch):
    groups, rows, _ = u_g.shape
    n_chunks = rows // batch
    gb = SSM_GB
    lsel = lambda a: pl.BlockSpec((gb,) + a.shape[1:],
                                  lambda i: (layer * (groups // gb) + i,) + (0,) * (a.ndim - 1))
    useg = pl.BlockSpec((gb, rows, SSM_LC), lambda i: (i, 0, 0))
    return pl.pallas_call(
        functools.partial(_ssm_body, batch=batch, n_chunks=n_chunks),
        grid=(groups // gb,),
        in_specs=[useg, lsel(toep), lsel(bs), lsel(cs), lsel(al), lsel(d_e)],
        out_specs=useg,
        out_shape=jax.ShapeDtypeStruct(u_g.shape, BF16),
        scratch_shapes=[pltpu.VMEM((gb, 2, rows, 2 * SSM_STATE), F32),
                        pltpu.VMEM((gb, rows, 2 * SSM_STATE), F32)],
        compiler_params=_params(("parallel",)),
        name="s5_scan",
    )(u_g, toep, bs, cs, al, d_e)


def _mla_body(q_ref, k_ref, vt_ref, szm_ref, o_ref, sa_ref, sb_ref, m_ref, acc_ref, *, tq):
    qi = pl.program_id(2)
    key = lax.broadcasted_iota(jnp.int32, (tq, tq), 0)
    qry = lax.broadcasted_iota(jnp.int32, (tq, tq), 1)
    causal = qry >= key
    heads = [slice(h * LANES, (h + 1) * LANES) for h in range(MLA_HB)]

    def scores(ki, dst_ref):
        k0 = pl.multiple_of(ki * tq, tq)
        for h, hl in enumerate(heads):
            dst_ref[h] = _dot_nt(k_ref[0, pl.ds(k0, tq), hl], q_ref[0, :, hl])

    def consume(ki, src_ref, masked):
        for h, hl in enumerate(heads):
            s = src_ref[h]
            if masked:
                s = jnp.where(causal, s, -1e30)
            m = m_ref[h]
            m_new = jnp.maximum(m, jnp.max(s, axis=0, keepdims=True))
            alpha = jnp.exp2(m - m_new)
            p = jnp.exp2((s - m_new).astype(BF16))
            acc_ref[h] = alpha * acc_ref[h] + _dot(vt_ref[0, ki, hl, :], p)
            m_ref[h] = m_new

    m_ref[...] = jnp.full(m_ref.shape, -1e30, F32)
    acc_ref[...] = jnp.zeros(acc_ref.shape, F32)
    scores(0, sa_ref)

    def pair(j, carry):
        scores(2 * j + 1, sb_ref)
        consume(2 * j, sa_ref, False)
        scores(2 * j + 2, sa_ref)
        consume(2 * j + 1, sb_ref, False)
        return carry

    lax.fori_loop(0, qi // 2, pair, 0)

    @pl.when(qi % 2 == 0)
    def _():
        consume(qi, sa_ref, True)

    @pl.when(qi % 2 == 1)
    def _():
        scores(qi, sb_ref)
        consume(qi - 1, sa_ref, False)
        consume(qi, sb_ref, True)

    lane = lax.broadcasted_iota(jnp.int32, (tq, LANES), 1)
    for hp in range(MLA_HB // 2):
        a0 = acc_ref[2 * hp]
        a1 = acc_ref[2 * hp + 1]
        o0 = (a0 / a0[MLA_V:MLA_V + 1, :]).T
        o1 = (a1 / a1[MLA_V:MLA_V + 1, :]).T
        o = jnp.where(lane < MLA_V, o0, pltpu.roll(o1, MLA_V, 1))
        sl = slice(hp * LANES, (hp + 1) * LANES)
        o_ref[0, :, sl] = (o * szm_ref[0, :, sl].astype(F32)).astype(BF16)


def _mla(q, k, vt, szm, tq):
    b, s, _ = q.shape
    hb = MLA_HB
    qk = lambda rows, blk: pl.BlockSpec((1, rows, hb * LANES), blk)
    outw = hb * MLA_V
    return pl.pallas_call(
        functools.partial(_mla_body, tq=tq),
        grid=(b, MLA_HEADS // hb, s // tq),
        in_specs=[qk(tq, lambda bi, hp, qi: (bi, qi, hp)),
                  qk(s, lambda bi, hp, qi: (bi, 0, hp)),
                  pl.BlockSpec((1, s // tq, hb * LANES, tq), lambda bi, hp, qi: (bi, 0, hp, 0)),
                  pl.BlockSpec((1, tq, outw), lambda bi, hp, qi: (bi, qi, hp))],
        out_specs=pl.BlockSpec((1, tq, outw), lambda bi, hp, qi: (bi, qi, hp)),
        out_shape=jax.ShapeDtypeStruct((b, s, MLA_WIDTH), BF16),
        scratch_shapes=[pltpu.VMEM((hb, tq, tq), F32), pltpu.VMEM((hb, tq, tq), F32),
                        pltpu.VMEM((hb, 1, tq), F32), pltpu.VMEM((hb, LANES, tq), F32)],
        compiler_params=_params(("parallel", "parallel", "arbitrary")),
        name="mla_attention",
    )(q, k, vt, szm)


def _memattn_body(q_ref, kv_ref, szx_ref, o_ref):
    scale = X_HEAD_DIM ** -0.5
    for h in range(X_HEADS):
        sl = slice(h * X_HEAD_DIM, (h + 1) * X_HEAD_DIM)
        q = q_ref[0, :, sl]
        k = kv_ref[0, 0, :, sl]
        v = kv_ref[0, 0, :, X_WIDTH + h * X_HEAD_DIM:X_WIDTH + (h + 1) * X_HEAD_DIM]
        s = _dot_nt(q, k) * scale
        p = jnp.exp(s - jnp.max(s, axis=1, keepdims=True))
        o = _dot(p.astype(BF16), v) / jnp.sum(p, axis=1, keepdims=True)
        o_ref[0, :, sl] = (o * szx_ref[0, :, sl].astype(F32)).astype(BF16)


def _memattn(qx, kvm, szx, layer, tm):
    b, s, _ = qx.shape
    m = kvm.shape[2]
    tile = pl.BlockSpec((1, tm, X_WIDTH), lambda bi, i: (bi, i, 0))
    return pl.pallas_call(
        _memattn_body,
        grid=(b, s // tm),
        in_specs=[tile, pl.BlockSpec((1, 1, m, 2 * X_WIDTH), lambda bi, i: (layer, bi, 0, 0)), tile],
        out_specs=tile,
        out_shape=jax.ShapeDtypeStruct((b, s, X_WIDTH), BF16),
        compiler_params=_params(("parallel", "parallel")),
        name="mem_attention",
    )(qx, kvm, szx)


def _merge_body(x_ref, g0_ref, gn_ref, szs_ref, ymla_ref, ymem_ref, wg_ref, bg_ref, wglu_ref, bglu_ref,
                pssm_ref, pmla_ref, pmem_ref, wout_ref, lng_ref, lnb_ref, o_ref, st_scr, gt_scr, *, alpha):
    tiles = SSM_WIDTH // LANES

    def to_token_major(g_ref):
        n_c = g_ref.shape[1]
        for g in range(SSM_GROUPS):
            blk = g_ref[g].astype(F32)
            for s in range(SSM_CHUNK):
                st_scr[s, :, g * SSM_GROUP:(g + 1) * SSM_GROUP] = blk[:, s * SSM_GROUP:(s + 1) * SSM_GROUP]
        for s in range(SSM_CHUNK):
            for j in range(tiles):
                gt_scr[j, pl.ds(s, n_c, stride=SSM_CHUNK), :] = st_scr[s, :, j * LANES:(j + 1) * LANES]

    @pl.when(pl.program_id(0) == 0)
    def _():
        to_token_major(g0_ref)

    x = x_ref[...]
    xb = x.astype(BF16)
    gt = jnp.concatenate([gt_scr[j] for j in range(tiles)], axis=1).astype(BF16)
    to_token_major(gn_ref)
    t = _dot(gt, wglu_ref[0]) + bglu_ref[0]
    y_ssm = (t[:, :SSM_WIDTH] * _sigmoid(t[:, SSM_WIDTH:]) * szs_ref[...].astype(F32)).astype(BF16)
    merged = None
    for i, (y, p_ref) in enumerate(((y_ssm, pssm_ref), (ymla_ref[...], pmla_ref), (ymem_ref[...], pmem_ref))):
        cols = slice(i * D_MODEL, (i + 1) * D_MODEL)
        gate = _sigmoid(_dot(xb, wg_ref[0, :, cols]) + bg_ref[0, :, cols])
        term = gate * _dot(y, p_ref[0])
        merged = term if merged is None else merged + term
    r = alpha * x + _dot(merged.astype(BF16), wout_ref[0])
    mu = jnp.mean(r, axis=-1, keepdims=True)
    var = jnp.mean(jnp.square(r - mu), axis=-1, keepdims=True)
    o_ref[...] = (r - mu) * lax.rsqrt(var + NORM_EPS) * lng_ref[0] + lnb_ref[0]


def _merge(x, g, szs, ymla, ymem, wg, bg, wglu, bglu, pssm, pmla, pmem, wout, lng, lnb, layer, tm, alpha):
    t = x.shape[0]
    row = lambda w: pl.BlockSpec((tm, w), lambda i: (i, 0))
    lsel = lambda a: pl.BlockSpec((1,) + a.shape[1:], lambda i: (layer,) + (0,) * (a.ndim - 1))
    weights = (wg, bg, wglu, bglu, pssm, pmla, pmem, wout, lng, lnb)
    n_c = tm // SSM_CHUNK
    last = t // tm - 1
    slab = (SSM_GROUPS, n_c, SSM_LC)
    return pl.pallas_call(
        functools.partial(_merge_body, alpha=alpha),
        grid=(t // tm,),
        in_specs=[row(D_MODEL),
                  pl.BlockSpec(slab, lambda i: (0, 0, 0)),
                  pl.BlockSpec(slab, lambda i: (0, jnp.minimum(i + 1, last), 0)),
                  row(SSM_WIDTH), row(MLA_WIDTH), row(X_WIDTH)]
        + [lsel(w) for w in weights],
        out_specs=row(D_MODEL),
        out_shape=jax.ShapeDtypeStruct((t, D_MODEL), F32),
        scratch_shapes=[pltpu.VMEM((SSM_CHUNK, n_c, SSM_WIDTH), F32),
                        pltpu.VMEM((SSM_WIDTH // LANES, tm, LANES), F32)],
        compiler_params=_params(("arbitrary",)),
        name="merge_out_ln",
    )(x, g, g, szs, ymla, ymem, *weights)


def _rotate_half_cols(w):
    half = w.shape[-1] // 2
    return jnp.concatenate([-w[..., half:], w[..., :half]], axis=-1)


def _pack_weights(w_in, w_uq, w_ukv):
    depth = w_in.shape[0]
    split = (SSM_WIDTH, SSM_WIDTH, MLA_Q_RANK, MLA_KV_RANK, MLA_ROPE, MLA_WIDTH, X_WIDTH, X_WIDTH)
    offs = [0]
    for w in split:
        offs.append(offs[-1] + w)
    seg = [w_in[:, :, offs[i]:offs[i + 1]] for i in range(len(split))]
    w_u, w_zs, w_cq, w_ckv, w_kr, w_zm, w_qx, w_zx = seg
    w_gate = w_in[:, :, offs[-1]:]

    def rope_tile(w):
        z = jnp.zeros(w.shape[:-1] + (MLA_NOPE,), w.dtype)
        z2 = jnp.zeros(w.shape[:-1] + (LANES - MLA_NOPE - MLA_ROPE,), w.dtype)
        return jnp.concatenate([z, w, z2], axis=-1)

    w1 = jnp.concatenate([w_u, w_zs, w_cq, w_ckv, rope_tile(w_kr), rope_tile(_rotate_half_cols(w_kr)),
                          w_zm, w_qx, w_zx], axis=-1).astype(BF16)

    uq = w_uq.reshape(depth, MLA_Q_RANK, MLA_HEADS, MLA_NOPE + MLA_ROPE)
    q_nope, q_rope = uq[..., :MLA_NOPE], uq[..., MLA_NOPE:]
    pad = jnp.zeros(q_rope.shape, uq.dtype)
    q_main = jnp.concatenate([q_nope, q_rope, pad], axis=-1)
    q_rot = jnp.concatenate([jnp.zeros(q_nope.shape, uq.dtype), _rotate_half_cols(q_rope), pad], axis=-1)
    wq = jnp.concatenate([q_main.reshape(depth, MLA_Q_RANK, _QW), q_rot.reshape(depth, MLA_Q_RANK, _QW)],
                         axis=-1).astype(BF16)

    ukv = w_ukv.reshape(depth, MLA_KV_RANK, MLA_HEADS, MLA_NOPE + MLA_V)
    k_nope, v = ukv[..., :MLA_NOPE], ukv[..., MLA_NOPE:]
    k_tile = jnp.concatenate([k_nope, jnp.zeros(k_nope.shape, ukv.dtype)], axis=-1)
    wk = k_tile.reshape(depth, MLA_KV_RANK, _QW).astype(BF16)
    v_t = jnp.transpose(v, (0, 2, 3, 1))
    wvt = jnp.concatenate([v_t, jnp.zeros(v_t.shape, ukv.dtype)], axis=2).reshape(depth, _QW, MLA_KV_RANK)
    return w1, w_gate.astype(BF16), wq, wk, wvt.astype(BF16)


def _pack_ssm_params(a_re, a_im, log_dt, b_re, b_im, c_re, c_im, d):
    depth = a_re.shape[0]
    n = depth * SSM_GROUPS
    ar = a_re.reshape(n, SSM_STATE)
    ai = a_im.reshape(n, SSM_STATE)
    ldt = jnp.broadcast_to(log_dt.reshape(n, 1, 1), (n, 1, 2 * SSM_STATE))
    a2 = jnp.stack([jnp.concatenate([ar, ar], -1), jnp.concatenate([ai, ai], -1)], axis=1)
    bt = jnp.stack([b_re.reshape(n, SSM_STATE, SSM_GROUP), b_im.reshape(n, SSM_STATE, SSM_GROUP)], axis=1)
    bt = jnp.swapaxes(bt, -1, -2)
    bt2 = jnp.concatenate([bt, bt], axis=-1)
    ct = jnp.stack([c_re.reshape(n, SSM_GROUP, SSM_STATE), c_im.reshape(n, SSM_GROUP, SSM_STATE)], axis=1)
    ct2 = jnp.concatenate([ct, ct], axis=-1)
    d_e = jnp.tile(d.reshape(n, 1, SSM_GROUP), (1, 1, SSM_CHUNK))
    return ldt, a2, bt2, ct2, d_e


def kernel(x, mem, positions, w_in, b_gate, ssm_a_re, ssm_a_im, ssm_log_dt, ssm_b_re, ssm_b_im, ssm_c_re,
           ssm_c_im, ssm_d, w_glu, b_glu, mla_q_norm, w_uq, mla_kv_norm, w_ukv, w_mem_kv, p_ssm, p_mla, p_mem,
           w_out, ln_g, ln_b):
    bsz, s, d = x.shape
    depth = w_in.shape[0]
    mlen = mem.shape[1]
    t = bsz * s
    assert d == D_MODEL and s % SSM_CHUNK == 0 and SSM_GROUPS % SSM_GB == 0
    tm = min(512, s)
    tq = min(512, s)
    assert s % tm == 0 and s % tq == 0
    alpha = (2 * depth) ** 0.25
    n_chunks = s // SSM_CHUNK

    w1, w_gate, wq, wk, wvt = _pack_weights(w_in, w_uq, w_ukv)
    row3 = lambda a: a.reshape(depth, 1, a.shape[-1])
    qn, kvn = row3(mla_q_norm), row3(mla_kv_norm)
    bg, bglu, lng, lnb = row3(b_gate), row3(b_glu), row3(ln_g), row3(ln_b)
    wglu, pssm, pmla, pmem, wout = (a.astype(BF16) for a in (w_glu, p_ssm, p_mla, p_mem, w_out))

    inv_freq = ROPE_THETA ** (-jnp.arange(0, MLA_ROPE, 2, dtype=F32) / MLA_ROPE)
    invf = jnp.concatenate([jnp.zeros((MLA_NOPE,), F32), inv_freq, inv_freq,
                            jnp.zeros((LANES - MLA_NOPE - MLA_ROPE,), F32)]).reshape(1, LANES)
    pos_b = jnp.broadcast_to(positions.astype(F32).reshape(t, 1), (t, LANES))
    cos, sin = _rope_tables(pos_b, invf, tm)

    kvm = _memkv(mem.reshape(bsz * mlen, d).astype(BF16), w_mem_kv.astype(BF16))
    kvm = kvm.reshape(depth, bsz, mlen, 2 * X_WIDTH)

    ldt, a2, bt2, ct2, d_e = _pack_ssm_params(ssm_a_re, ssm_a_im, ssm_log_dt, ssm_b_re, ssm_b_im,
                                              ssm_c_re, ssm_c_im, ssm_d)
    toep, bs, cs, al = _ssm_prep(ldt, a2, bt2, ct2)

    xf = x.reshape(t, d)
    for layer in range(depth):
        u_g, vt, szs, q, k, szm, qx, szx = _inproj(xf, w1, wq, wk, wvt, qn, kvn, cos, sin, layer, tm, s)
        g = _ssm(u_g, toep, bs, cs, al, d_e, layer, bsz)
        y_mla = _mla(q.reshape(bsz, s, _QW), k.reshape(bsz, s, _QW), vt, szm.reshape(bsz, s, MLA_WIDTH), tq)
        y_mem = _memattn(qx.reshape(bsz, s, X_WIDTH), kvm, szx.reshape(bsz, s, X_WIDTH), layer, tm)

        xf = _merge(xf, g, szs, y_mla.reshape(t, MLA_WIDTH), y_mem.reshape(t, X_WIDTH), w_gate, bg, wglu, bglu,
                    pssm, pmla, pmem, wout, lng, lnb, layer, tm, alpha)
    return xf.reshape(bsz, s, d)
```

```python
import functools
import math

import jax
import jax.numpy as jnp
from jax import lax
from jax.experimental import pallas as pl
from jax.experimental.pallas import tpu as pltpu

F32 = jnp.float32
BF16 = jnp.bfloat16

LANES = 128
D_MODEL = 1024
SSM_WIDTH = 512
SSM_GROUP = 16
SSM_GROUPS = SSM_WIDTH // SSM_GROUP
SSM_STATE = 64
MLA_HEADS = 8
MLA_NOPE = 64
MLA_ROPE = 32
MLA_V = 64
MLA_Q_RANK = 256
MLA_KV_RANK = 128
MLA_WIDTH = MLA_HEADS * MLA_V
ROPE_THETA = 10000.0
X_HEADS = 4
X_HEAD_DIM = 128
X_WIDTH = X_HEADS * X_HEAD_DIM
N_BRANCH = 3
NORM_EPS = 1e-5

SSM_CHUNK = 32
SSM_LC = SSM_CHUNK * SSM_GROUP
SSM_GB = 4
MLA_HB = 4
VMEM_LIMIT = 48 * 1024 * 1024


def _sigmoid(v):
    return jax.nn.sigmoid(v)


def _silu(v):
    return v * jax.nn.sigmoid(v)


def _dot(a, b):
    return jnp.dot(a, b, preferred_element_type=F32)


def _dot_nt(a, b):
    return lax.dot_general(a, b, (((1,), (1,)), ((), ())), preferred_element_type=F32)


def _params(sem):
    return pltpu.CompilerParams(dimension_semantics=sem, vmem_limit_bytes=VMEM_LIMIT)


def _rope_body(pos_ref, invf_ref, c_ref, s_ref):
    ang = pos_ref[...] * invf_ref[...]
    lane = lax.broadcasted_iota(jnp.int32, ang.shape, 1)
    rope = (lane >= MLA_NOPE) & (lane < MLA_NOPE + MLA_ROPE)
    c_ref[...] = jnp.where(lane < MLA_NOPE, 1.0, jnp.where(rope, jnp.cos(ang), 0.0))
    s_ref[...] = jnp.where(rope, jnp.sin(ang), 0.0)


def _rope_tables(pos_b, invf, tm):
    t = pos_b.shape[0]
    spec = pl.BlockSpec((tm, LANES), lambda i: (i, 0))
    return pl.pallas_call(
        _rope_body,
        grid=(t // tm,),
        in_specs=[spec, pl.BlockSpec((1, LANES), lambda i: (0, 0))],
        out_specs=[spec, spec],
        out_shape=[jax.ShapeDtypeStruct((t, LANES), F32)] * 2,
        compiler_params=_params(("parallel",)),
        name="rope_tables",
    )(pos_b, invf)


def _memkv_body(mem_ref, w_ref, o_ref):
    o_ref[0] = _dot(mem_ref[...], w_ref[0]).astype(BF16)


def _memkv(mem_b, w_b):
    depth = w_b.shape[0]
    rows = mem_b.shape[0]
    return pl.pallas_call(
        _memkv_body,
        grid=(depth,),
        in_specs=[pl.BlockSpec((rows, D_MODEL), lambda l: (0, 0)),
                  pl.BlockSpec((1, D_MODEL, 2 * X_WIDTH), lambda l: (l, 0, 0))],
        out_specs=pl.BlockSpec((1, rows, 2 * X_WIDTH), lambda l: (l, 0, 0)),
        out_shape=jax.ShapeDtypeStruct((depth, rows, 2 * X_WIDTH), BF16),
        compiler_params=_params(("parallel",)),
        name="mem_kv",
    )(mem_b, w_b)


_C_U = 0
_C_ZS = _C_U + SSM_WIDTH
_C_CQ = _C_ZS + SSM_WIDTH
_C_CKV = _C_CQ + MLA_Q_RANK
_C_KR = _C_CKV + MLA_KV_RANK
_C_ZM = _C_KR + 2 * LANES
_C_QX = _C_ZM + MLA_WIDTH
_C_ZX = _C_QX + X_WIDTH
_C_END = _C_ZX + X_WIDTH
_QW = MLA_HEADS * LANES


def _rms(v, g):
    return v * lax.rsqrt(jnp.mean(jnp.square(v), axis=-1, keepdims=True) + NORM_EPS) * g


def _inproj_body(x_ref, w1_ref, wq_ref, wk_ref, wvt_ref, qn_ref, kvn_ref, c_ref, s_ref, kvm_ref,
                 u_ref, vt_ref, szs_ref, q_ref, k_ref, szm_ref, ymem_ref, u_scr):
    xb = x_ref[...].astype(BF16)

    def proj(c0, c1):
        return _dot(xb, w1_ref[0, :, c0:c1])

    u = proj(_C_U, _C_ZS)
    n_c = u_scr.shape[1] // SSM_CHUNK
    gpt = LANES // SSM_GROUP
    for j in range(SSM_WIDTH // LANES):
        u_scr[j] = u[:, j * LANES:(j + 1) * LANES]
    for s in range(SSM_CHUNK):
        for j in range(SSM_WIDTH // LANES):
            a = u_scr[j, pl.ds(s, n_c, stride=SSM_CHUNK), :].astype(BF16)
            for g in range(gpt):
                u_ref[j * gpt + g, :, s * SSM_GROUP:(s + 1) * SSM_GROUP] = a[:, g * SSM_GROUP:(g + 1) * SSM_GROUP]
    szs_ref[...] = _silu(proj(_C_ZS, _C_CQ)).astype(BF16)
    szm_ref[...] = _silu(proj(_C_ZM, _C_QX)).astype(BF16)

    qx = proj(_C_QX, _C_ZX).astype(BF16)
    szx = _silu(proj(_C_ZX, _C_END))
    for h in range(X_HEADS):
        sl = slice(h * X_HEAD_DIM, (h + 1) * X_HEAD_DIM)
        km = kvm_ref[0, 0, :, sl]
        vm = kvm_ref[0, 0, :, X_WIDTH + h * X_HEAD_DIM:X_WIDTH + (h + 1) * X_HEAD_DIM]
        sc = _dot_nt(qx[:, sl], km) * X_HEAD_DIM ** -0.5
        pm = jnp.exp(sc - jnp.max(sc, axis=1, keepdims=True))
        om = _dot(pm.astype(BF16), vm) / jnp.sum(pm, axis=1, keepdims=True)
        ymem_ref[:, sl] = (om * szx[:, sl]).astype(BF16)

    cos = c_ref[...]
    sin = s_ref[...]
    scale = (MLA_NOPE + MLA_ROPE) ** -0.5 * math.log2(math.e)
    cos_q = cos * scale
    sin_q = sin * scale

    cq = _rms(proj(_C_CQ, _C_CKV), qn_ref[0]).astype(BF16)
    qa = _dot(cq, wq_ref[0])
    for h in range(MLA_HEADS):
        a = qa[:, h * LANES:(h + 1) * LANES]
        b = qa[:, _QW + h * LANES:_QW + (h + 1) * LANES]
        q_ref[:, h * LANES:(h + 1) * LANES] = (a * cos_q + b * sin_q).astype(BF16)

    ckv = _rms(proj(_C_CKV, _C_KR), kvn_ref[0]).astype(BF16)
    kn = _dot(ckv, wk_ref[0])
    kr = proj(_C_KR, _C_ZM)
    kr = kr[:, :LANES] * cos + kr[:, LANES:] * sin
    for h in range(MLA_HEADS):
        k_ref[:, h * LANES:(h + 1) * LANES] = (kn[:, h * LANES:(h + 1) * LANES] + kr).astype(BF16)
    sub = lax.broadcasted_iota(jnp.int32, (_QW, 1), 0)
    ones_row = jnp.where(sub % LANES == MLA_V, 1.0, 0.0)
    vt_ref[0, 0] = (_dot_nt(wvt_ref[0], ckv) + ones_row).astype(BF16)


def _inproj(x, w1, wq, wk, wvt, qn, kvn, cos, sin, kvm, layer, tm, seq):
    t = x.shape[0]
    n_c = tm // SSM_CHUNK
    nblk = seq // tm
    row = lambda w: pl.BlockSpec((tm, w), lambda i: (i, 0))
    lsel = lambda a: pl.BlockSpec((1,) + a.shape[1:], lambda i: (layer,) + (0,) * (a.ndim - 1))
    widths = (SSM_WIDTH, _QW, _QW, MLA_WIDTH, X_WIDTH)
    slab = (SSM_GROUPS, n_c, SSM_LC)
    return pl.pallas_call(
        _inproj_body,
        grid=(t // tm,),
        in_specs=[row(D_MODEL), lsel(w1), lsel(wq), lsel(wk), lsel(wvt), lsel(qn), lsel(kvn),
                  row(LANES), row(LANES),
                  pl.BlockSpec((1, 1) + kvm.shape[2:], lambda i: (layer, i // nblk, 0, 0))],
        out_specs=[pl.BlockSpec(slab, lambda i: (0, i, 0)),
                   pl.BlockSpec((1, 1, _QW, tm), lambda i: (i // nblk, i % nblk, 0, 0))]
        + [row(w) for w in widths],
        out_shape=[jax.ShapeDtypeStruct((SSM_GROUPS, t // SSM_CHUNK, SSM_LC), BF16),
                   jax.ShapeDtypeStruct((t // seq, nblk, _QW, tm), BF16)]
        + [jax.ShapeDtypeStruct((t, w), BF16) for w in widths],
        scratch_shapes=[pltpu.VMEM((SSM_WIDTH // LANES, tm, LANES), F32)],
        compiler_params=_params(("parallel",)),
        name="in_proj",
    )(x, w1, wq, wk, wvt, qn, kvn, cos, sin, kvm)


def _ssm_prep_body(ldt_ref, a2_ref, bt2_ref, ct2_ref, toep_ref, bs_ref, cs_ref, al_ref):
    chunk = SSM_CHUNK
    dt = jnp.exp(ldt_ref[0])
    lr2 = a2_ref[0, 0:1, :]
    li2 = a2_ref[0, 1:2, :]
    lane2 = lax.broadcasted_iota(jnp.int32, (1, 2 * SSM_STATE), 1)
    first = lane2 < SSM_STATE

    def power(e):
        m = jnp.exp(lr2 * dt * e)
        return m * jnp.cos(li2 * dt * e), m * jnp.sin(li2 * dt * e)

    def per_row(v, reps):
        return jnp.broadcast_to(v[:, None, :], (v.shape[0], reps, v.shape[1])).reshape(v.shape[0] * reps, v.shape[1])

    lb_re, lb_im = power(1.0)
    nr, ni = lb_re - 1.0, lb_im
    den = lr2 * lr2 + li2 * li2
    f_re = (nr * lr2 + ni * li2) / den
    f_im = (ni * lr2 - nr * li2) / den
    bt_re = bt2_ref[0, 0]
    bt_im = bt2_ref[0, 1]
    bb_re = f_re * bt_re - f_im * bt_im
    bb_im = f_re * bt_im + f_im * bt_re

    kcol = lax.broadcasted_iota(jnp.int32, (chunk, 1), 0).astype(F32)
    pk_re, pk_im = power(kcol)
    pe_re, pe_im = power(chunk - 1.0 - kcol)

    pr, pi = per_row(pe_re, SSM_GROUP), per_row(pe_im, SSM_GROUP)
    br = jnp.tile(bb_re, (chunk, 1))
    bi = jnp.tile(bb_im, (chunk, 1))
    s_re = pr * br - pi * bi
    s_im = pr * bi + pi * br
    bs_ref[0, :, :2 * SSM_STATE] = jnp.where(first, s_re, s_im).astype(BF16)
    bs_ref[0, :, 2 * SSM_STATE:] = jnp.where(first, s_im, s_re).astype(BF16)

    qr, qi = per_row(pk_re, SSM_GROUP), per_row(pk_im, SSM_GROUP)
    c_re = jnp.tile(ct2_ref[0, 0], (chunk, 1))
    c_im = jnp.tile(ct2_ref[0, 1], (chunk, 1))
    m_re = c_re * qr - c_im * qi
    m_im = c_re * qi + c_im * qr
    hi = lax.Precision.HIGHEST
    nt = (((1,), (1,)), ((), ()))
    kcat = (lax.dot_general(bb_re, jnp.where(first, m_re, 0.0), nt, precision=hi, preferred_element_type=F32)
            - lax.dot_general(bb_im, jnp.where(first, m_im, 0.0), nt, precision=hi, preferred_element_type=F32))
    lane = lax.broadcasted_iota(jnp.int32, (SSM_GROUP, SSM_LC), 1)
    for s in range(chunk):
        shifted = kcat if s == 0 else pltpu.roll(kcat, SSM_GROUP * s, 1)
        toep_ref[0, s * SSM_GROUP:(s + 1) * SSM_GROUP, :] = jnp.where(
            lane >= SSM_GROUP * s, shifted, 0.0).astype(BF16)

    n_re = m_re * lb_re - m_im * lb_im
    n_im = m_re * lb_im + m_im * lb_re
    cs_ref[0] = jnp.where(first, n_re, -n_im).T.astype(BF16)

    a_re, a_im = power(float(chunk))
    al_ref[0, 0:1, :] = a_re
    al_ref[0, 1:2, :] = jnp.where(first, -a_im, a_im)


def _ssm_prep(ldt, a2, bt2, ct2):
    n = ldt.shape[0]
    sel = lambda a: pl.BlockSpec((1,) + a.shape[1:], lambda i: (i,) + (0,) * (a.ndim - 1))
    out_shapes = [jax.ShapeDtypeStruct((n, SSM_LC, SSM_LC), BF16),
                  jax.ShapeDtypeStruct((n, SSM_LC, 4 * SSM_STATE), BF16),
                  jax.ShapeDtypeStruct((n, 2 * SSM_STATE, SSM_LC), BF16),
                  jax.ShapeDtypeStruct((n, 2, 2 * SSM_STATE), F32)]
    return pl.pallas_call(
        _ssm_prep_body,
        grid=(n,),
        in_specs=[sel(ldt), sel(a2), sel(bt2), sel(ct2)],
        out_specs=[sel(s) for s in out_shapes],
        out_shape=out_shapes,
        compiler_params=_params(("parallel",)),
        name="ssm_prep",
    )(ldt, a2, bt2, ct2)


def _ssm_body(u_ref, toep_ref, bs_ref, cs_ref, al_ref, d_ref, o_ref, hloc_ref, hst_ref, *, batch, n_chunks):
    gb = u_ref.shape[0]
    w = 2 * SSM_STATE
    for j in range(gb):
        hl = _dot(u_ref[j], bs_ref[j])
        hloc_ref[j, 0] = hl[:, :w]
        hloc_ref[j, 1] = hl[:, w:]

    a1 = [jnp.broadcast_to(al_ref[j, 0:1, :], (batch, 2 * SSM_STATE)) for j in range(gb)]
    a2 = [jnp.broadcast_to(al_ref[j, 1:2, :], (batch, 2 * SSM_STATE)) for j in range(gb)]

    def step(c, carry):
        out = []
        rows = pl.ds(c, batch, stride=n_chunks)
        for j in range(gb):
            h, hs = carry[j]
            hst_ref[j, rows, :] = h
            out.append((a1[j] * h + a2[j] * hs + hloc_ref[j, 0, rows, :],
                        a1[j] * hs - a2[j] * h + hloc_ref[j, 1, rows, :]))
        return tuple(out)

    zero = jnp.zeros((batch, 2 * SSM_STATE), F32)
    lax.fori_loop(0, n_chunks, step, tuple((zero, zero) for _ in range(gb)), unroll=4)

    for j in range(gb):
        u = u_ref[j]
        y = (_dot(u, toep_ref[j]) + _dot(hst_ref[j].astype(BF16), cs_ref[j])
             + d_ref[j] * u.astype(F32))
        o_ref[j] = jax.nn.gelu(y).astype(BF16)


def _ssm(u_g, toep, bs, cs, al, d_e, layer, batch):
    groups, rows, _ = u_g.shape
    n_chunks = rows // batch
    gb = SSM_GB
    lsel = lambda a: pl.BlockSpec((gb,) + a.shape[1:],
                                  lambda i: (layer * (groups // gb) + i,) + (0,) * (a.ndim - 1))
    useg = pl.BlockSpec((gb, rows, SSM_LC), lambda i: (i, 0, 0))
    return pl.pallas_call(
        functools.partial(_ssm_body, batch=batch, n_chunks=n_chunks),
        grid=(groups // gb,),
        in_specs=[useg, lsel(toep), lsel(bs), lsel(cs), lsel(al), lsel(d_e)],
        out_specs=useg,
        out_shape=jax.ShapeDtypeStruct(u_g.shape, BF16),
        scratch_shapes=[pltpu.VMEM((gb, 2, rows, 2 * SSM_STATE), F32),
                        pltpu.VMEM((gb, rows, 2 * SSM_STATE), F32)],
        compiler_params=_params(("parallel",)),
        name="s5_scan",
    )(u_g, toep, bs, cs, al, d_e)


def _mla_body(q_ref, k_ref, vt_ref, szm_ref, o_ref, sa_ref, sb_ref, m_ref, acc_ref, *, tq):
    qi = pl.program_id(2)
    key = lax.broadcasted_iota(jnp.int32, (tq, tq), 0)
    qry = lax.broadcasted_iota(jnp.int32, (tq, tq), 1)
    causal = qry >= key
    heads = [slice(h * LANES, (h + 1) * LANES) for h in range(MLA_HB)]

    def scores(ki, dst_ref):
        k0 = pl.multiple_of(ki * tq, tq)
        for h, hl in enumerate(heads):
            dst_ref[h] = _dot_nt(k_ref[0, pl.ds(k0, tq), hl], q_ref[0, :, hl])

    def consume(ki, src_ref, masked):
        for h, hl in enumerate(heads):
            s = src_ref[h]
            if masked:
                s = jnp.where(causal, s, -1e30)
            m = m_ref[h]
            m_new = jnp.maximum(m, jnp.max(s, axis=0, keepdims=True))
            alpha = jnp.exp2(m - m_new)
            p = jnp.exp2((s - m_new).astype(BF16))
            acc_ref[h] = alpha * acc_ref[h] + _dot(vt_ref[0, ki, hl, :], p)
            m_ref[h] = m_new

    m_ref[...] = jnp.full(m_ref.shape, -1e30, F32)
    acc_ref[...] = jnp.zeros(acc_ref.shape, F32)
    scores(0, sa_ref)

    def pair(j, carry):
        scores(2 * j + 1, sb_ref)
        consume(2 * j, sa_ref, False)
        scores(2 * j + 2, sa_ref)
        consume(2 * j + 1, sb_ref, False)
        return carry

    lax.fori_loop(0, qi // 2, pair, 0)

    @pl.when(qi % 2 == 0)
    def _():
        consume(qi, sa_ref, True)

    @pl.when(qi % 2 == 1)
    def _():
        scores(qi, sb_ref)
        consume(qi - 1, sa_ref, False)
        consume(qi, sb_ref, True)

    lane = lax.broadcasted_iota(jnp.int32, (tq, LANES), 1)
    for hp in range(MLA_HB // 2):
        a0 = acc_ref[2 * hp]
        a1 = acc_ref[2 * hp + 1]
        o0 = (a0 / a0[MLA_V:MLA_V + 1, :]).T
        o1 = (a1 / a1[MLA_V:MLA_V + 1, :]).T
        o = jnp.where(lane < MLA_V, o0, pltpu.roll(o1, MLA_V, 1))
        sl = slice(hp * LANES, (hp + 1) * LANES)
        o_ref[0, :, sl] = (o * szm_ref[0, :, sl].astype(F32)).astype(BF16)


def _mla(q, k, vt, szm, tq):
    b, s, _ = q.shape
    hb = MLA_HB
    qk = lambda rows, blk: pl.BlockSpec((1, rows, hb * LANES), blk)
    outw = hb * MLA_V
    return pl.pallas_call(
        functools.partial(_mla_body, tq=tq),
        grid=(b, MLA_HEADS // hb, s // tq),
        in_specs=[qk(tq, lambda bi, hp, qi: (bi, qi, hp)),
                  qk(s, lambda bi, hp, qi: (bi, 0, hp)),
                  pl.BlockSpec((1, s // tq, hb * LANES, tq), lambda bi, hp, qi: (bi, 0, hp, 0)),
                  pl.BlockSpec((1, tq, outw), lambda bi, hp, qi: (bi, qi, hp))],
        out_specs=pl.BlockSpec((1, tq, outw), lambda bi, hp, qi: (bi, qi, hp)),
        out_shape=jax.ShapeDtypeStruct((b, s, MLA_WIDTH), BF16),
        scratch_shapes=[pltpu.VMEM((hb, tq, tq), F32), pltpu.VMEM((hb, tq, tq), F32),
                        pltpu.VMEM((hb, 1, tq), F32), pltpu.VMEM((hb, LANES, tq), F32)],
        compiler_params=_params(("parallel", "parallel", "arbitrary")),
        name="mla_attention",
    )(q, k, vt, szm)


def _merge_body(x_ref, g0_ref, gn_ref, szs_ref, ymla_ref, ymem_ref, wg_ref, bg_ref, wglu_ref, bglu_ref,
                pssm_ref, pmla_ref, pmem_ref, wout_ref, lng_ref, lnb_ref, o_ref, gt_scr, *, alpha):
    tiles = SSM_WIDTH // LANES

    def to_token_major(g_ref):
        n_c = g_ref.shape[1]
        gpt = LANES // SSM_GROUP
        lane = lax.broadcasted_iota(jnp.int32, (n_c, LANES), 1)
        piece = [(lane >= i * SSM_GROUP) & (lane < (i + 1) * SSM_GROUP) for i in range(gpt)]
        for j in range(tiles):
            for st in range(SSM_CHUNK // gpt):
                src = [g_ref[j * gpt + i, :, st * LANES:(st + 1) * LANES].astype(F32) for i in range(gpt)]
                for so in range(gpt):
                    merged = None
                    for i in range(gpt):
                        shift = ((i - so) * SSM_GROUP) % LANES
                        r = src[i] if shift == 0 else pltpu.roll(src[i], shift, 1)
                        merged = r if merged is None else jnp.where(piece[i], r, merged)
                    gt_scr[j, pl.ds(st * gpt + so, n_c, stride=SSM_CHUNK), :] = merged

    @pl.when(pl.program_id(0) == 0)
    def _():
        to_token_major(g0_ref)

    x = x_ref[...]
    xb = x.astype(BF16)
    gt = jnp.concatenate([gt_scr[j] for j in range(tiles)], axis=1).astype(BF16)
    to_token_major(gn_ref)
    t = _dot(gt, wglu_ref[0]) + bglu_ref[0]
    y_ssm = (t[:, :SSM_WIDTH] * _sigmoid(t[:, SSM_WIDTH:]) * szs_ref[...].astype(F32)).astype(BF16)
    merged = None
    for i, (y, p_ref) in enumerate(((y_ssm, pssm_ref), (ymla_ref[...], pmla_ref), (ymem_ref[...], pmem_ref))):
        cols = slice(i * D_MODEL, (i + 1) * D_MODEL)
        gate = _sigmoid(_dot(xb, wg_ref[0, :, cols]) + bg_ref[0, :, cols])
        term = gate * _dot(y, p_ref[0])
        merged = term if merged is None else merged + term
    r = alpha * x + _dot(merged.astype(BF16), wout_ref[0])
    mu = jnp.mean(r, axis=-1, keepdims=True)
    var = jnp.mean(jnp.square(r - mu), axis=-1, keepdims=True)
    o_ref[...] = (r - mu) * lax.rsqrt(var + NORM_EPS) * lng_ref[0] + lnb_ref[0]


def _merge(x, g, szs, ymla, ymem, wg, bg, wglu, bglu, pssm, pmla, pmem, wout, lng, lnb, layer, tm, alpha):
    t = x.shape[0]
    row = lambda w: pl.BlockSpec((tm, w), lambda i: (i, 0))
    lsel = lambda a: pl.BlockSpec((1,) + a.shape[1:], lambda i: (layer,) + (0,) * (a.ndim - 1))
    weights = (wg, bg, wglu, bglu, pssm, pmla, pmem, wout, lng, lnb)
    n_c = tm // SSM_CHUNK
    last = t // tm - 1
    slab = (SSM_GROUPS, n_c, SSM_LC)
    return pl.pallas_call(
        functools.partial(_merge_body, alpha=alpha),
        grid=(t // tm,),
        in_specs=[row(D_MODEL),
                  pl.BlockSpec(slab, lambda i: (0, 0, 0)),
                  pl.BlockSpec(slab, lambda i: (0, jnp.minimum(i + 1, last), 0)),
                  row(SSM_WIDTH), row(MLA_WIDTH), row(X_WIDTH)]
        + [lsel(w) for w in weights],
        out_specs=row(D_MODEL),
        out_shape=jax.ShapeDtypeStruct((t, D_MODEL), F32),
        scratch_shapes=[pltpu.VMEM((SSM_WIDTH // LANES, tm, LANES), F32)],
        compiler_params=_params(("arbitrary",)),
        name="merge_out_ln",
    )(x, g, g, szs, ymla, ymem, *weights)


def _rotate_half_cols(w):
    half = w.shape[-1] // 2
    return jnp.concatenate([-w[..., half:], w[..., :half]], axis=-1)


def _pack_weights(w_in, w_uq, w_ukv):
    depth = w_in.shape[0]
    split = (SSM_WIDTH, SSM_WIDTH, MLA_Q_RANK, MLA_KV_RANK, MLA_ROPE, MLA_WIDTH, X_WIDTH, X_WIDTH)
    offs = [0]
    for w in split:
        offs.append(offs[-1] + w)
    seg = [w_in[:, :, offs[i]:offs[i + 1]] for i in range(len(split))]
    w_u, w_zs, w_cq, w_ckv, w_kr, w_zm, w_qx, w_zx = seg
    w_gate = w_in[:, :, offs[-1]:]

    def rope_tile(w):
        z = jnp.zeros(w.shape[:-1] + (MLA_NOPE,), w.dtype)
        z2 = jnp.zeros(w.shape[:-1] + (LANES - MLA_NOPE - MLA_ROPE,), w.dtype)
        return jnp.concatenate([z, w, z2], axis=-1)

    w1 = jnp.concatenate([w_u, w_zs, w_cq, w_ckv, rope_tile(w_kr), rope_tile(_rotate_half_cols(w_kr)),
                          w_zm, w_qx, w_zx], axis=-1).astype(BF16)

    uq = w_uq.reshape(depth, MLA_Q_RANK, MLA_HEADS, MLA_NOPE + MLA_ROPE)
    q_nope, q_rope = uq[..., :MLA_NOPE], uq[..., MLA_NOPE:]
    pad = jnp.zeros(q_rope.shape, uq.dtype)
    q_main = jnp.concatenate([q_nope, q_rope, pad], axis=-1)
    q_rot = jnp.concatenate([jnp.zeros(q_nope.shape, uq.dtype), _rotate_half_cols(q_rope), pad], axis=-1)
    wq = jnp.concatenate([q_main.reshape(depth, MLA_Q_RANK, _QW), q_rot.reshape(depth, MLA_Q_RANK, _QW)],
                         axis=-1).astype(BF16)

    ukv = w_ukv.reshape(depth, MLA_KV_RANK, MLA_HEADS, MLA_NOPE + MLA_V)
    k_nope, v = ukv[..., :MLA_NOPE], ukv[..., MLA_NOPE:]
    k_tile = jnp.concatenate([k_nope, jnp.zeros(k_nope.shape, ukv.dtype)], axis=-1)
    wk = k_tile.reshape(depth, MLA_KV_RANK, _QW).astype(BF16)
    v_t = jnp.transpose(v, (0, 2, 3, 1))
    wvt = jnp.concatenate([v_t, jnp.zeros(v_t.shape, ukv.dtype)], axis=2).reshape(depth, _QW, MLA_KV_RANK)
    return w1, w_gate.astype(BF16), wq, wk, wvt.astype(BF16)


def _pack_ssm_params(a_re, a_im, log_dt, b_re, b_im, c_re, c_im, d):
    depth = a_re.shape[0]
    n = depth * SSM_GROUPS
    ar = a_re.reshape(n, SSM_STATE)
    ai = a_im.reshape(n, SSM_STATE)
    ldt = jnp.broadcast_to(log_dt.reshape(n, 1, 1), (n, 1, 2 * SSM_STATE))
    a2 = jnp.stack([jnp.concatenate([ar, ar], -1), jnp.concatenate([ai, ai], -1)], axis=1)
    bt = jnp.stack([b_re.reshape(n, SSM_STATE, SSM_GROUP), b_im.reshape(n, SSM_STATE, SSM_GROUP)], axis=1)
    bt = jnp.swapaxes(bt, -1, -2)
    bt2 = jnp.concatenate([bt, bt], axis=-1)
    ct = jnp.stack([c_re.reshape(n, SSM_GROUP, SSM_STATE), c_im.reshape(n, SSM_GROUP, SSM_STATE)], axis=1)
    ct2 = jnp.concatenate([ct, ct], axis=-1)
    d_e = jnp.tile(d.reshape(n, 1, SSM_GROUP), (1, 1, SSM_CHUNK))
    return ldt, a2, bt2, ct2, d_e


def kernel(x, mem, positions, w_in, b_gate, ssm_a_re, ssm_a_im, ssm_log_dt, ssm_b_re, ssm_b_im, ssm_c_re,
           ssm_c_im, ssm_d, w_glu, b_glu, mla_q_norm, w_uq, mla_kv_norm, w_ukv, w_mem_kv, p_ssm, p_mla, p_mem,
           w_out, ln_g, ln_b):
    bsz, s, d = x.shape
    depth = w_in.shape[0]
    mlen = mem.shape[1]
    t = bsz * s
    assert d == D_MODEL and s % SSM_CHUNK == 0 and SSM_GROUPS % SSM_GB == 0
    tm = min(512, s)
    tq = min(512, s)
    assert s % tm == 0 and s % tq == 0
    alpha = (2 * depth) ** 0.25
    n_chunks = s // SSM_CHUNK

    w1, w_gate, wq, wk, wvt = _pack_weights(w_in, w_uq, w_ukv)
    row3 = lambda a: a.reshape(depth, 1, a.shape[-1])
    qn, kvn = row3(mla_q_norm), row3(mla_kv_norm)
    bg, bglu, lng, lnb = row3(b_gate), row3(b_glu), row3(ln_g), row3(ln_b)
    wglu, pssm, pmla, pmem, wout = (a.astype(BF16) for a in (w_glu, p_ssm, p_mla, p_mem, w_out))

    inv_freq = ROPE_THETA ** (-jnp.arange(0, MLA_ROPE, 2, dtype=F32) / MLA_ROPE)
    invf = jnp.concatenate([jnp.zeros((MLA_NOPE,), F32), inv_freq, inv_freq,
                            jnp.zeros((LANES - MLA_NOPE - MLA_ROPE,), F32)]).reshape(1, LANES)
    pos_b = jnp.broadcast_to(positions.astype(F32).reshape(t, 1), (t, LANES))
    cos, sin = _rope_tables(pos_b, invf, tm)

    kvm = _memkv(mem.reshape(bsz * mlen, d).astype(BF16), w_mem_kv.astype(BF16))
    kvm = kvm.reshape(depth, bsz, mlen, 2 * X_WIDTH)

    ldt, a2, bt2, ct2, d_e = _pack_ssm_params(ssm_a_re, ssm_a_im, ssm_log_dt, ssm_b_re, ssm_b_im,
                                              ssm_c_re, ssm_c_im, ssm_d)
    toep, bs, cs, al = _ssm_prep(ldt, a2, bt2, ct2)

    xf = x.reshape(t, d)
    for layer in range(depth):
        u_g, vt, szs, q, k, szm, y_mem = _inproj(xf, w1, wq, wk, wvt, qn, kvn, cos, sin, kvm, layer, tm, s)
        g = _ssm(u_g, toep, bs, cs, al, d_e, layer, bsz)
        y_mla = _mla(q.reshape(bsz, s, _QW), k.reshape(bsz, s, _QW), vt, szm.reshape(bsz, s, MLA_WIDTH), tq)
        xf = _merge(xf, g, szs, y_mla.reshape(t, MLA_WIDTH), y_mem, w_gate, bg, wglu, bglu,
                    pssm, pmla, pmem, wout, lng, lnb, layer, tm, alpha)
    return xf.reshape(bsz, s, d)
```

```python
import functools
import math

import jax
import jax.numpy as jnp
from jax import lax
from jax.experimental import pallas as pl
from jax.experimental.pallas import tpu as pltpu

F32 = jnp.float32
BF16 = jnp.bfloat16

LANES = 128
D_MODEL = 1024
SSM_WIDTH = 512
SSM_GROUP = 16
SSM_GROUPS = SSM_WIDTH // SSM_GROUP
SSM_STATE = 64
MLA_HEADS = 8
MLA_NOPE = 64
MLA_ROPE = 32
MLA_V = 64
MLA_Q_RANK = 256
MLA_KV_RANK = 128
MLA_WIDTH = MLA_HEADS * MLA_V
ROPE_THETA = 10000.0
X_HEADS = 4
X_HEAD_DIM = 128
X_WIDTH = X_HEADS * X_HEAD_DIM
N_BRANCH = 3
NORM_EPS = 1e-5

SSM_CHUNK = 32
SSM_LC = SSM_CHUNK * SSM_GROUP
SSM_GB = 4
MLA_HB = 4
VMEM_LIMIT = 48 * 1024 * 1024


def _sigmoid(v):
    return jax.nn.sigmoid(v)


def _silu(v):
    return v * jax.nn.sigmoid(v)


def _dot(a, b):
    return jnp.dot(a, b, preferred_element_type=F32)


def _dot_nt(a, b):
    return lax.dot_general(a, b, (((1,), (1,)), ((), ())), preferred_element_type=F32)


def _params(sem):
    return pltpu.CompilerParams(dimension_semantics=sem, vmem_limit_bytes=VMEM_LIMIT)


def _rope_body(pos_ref, invf_ref, c_ref, s_ref):
    ang = pos_ref[...] * invf_ref[...]
    lane = lax.broadcasted_iota(jnp.int32, ang.shape, 1)
    rope = (lane >= MLA_NOPE) & (lane < MLA_NOPE + MLA_ROPE)
    c_ref[...] = jnp.where(lane < MLA_NOPE, 1.0, jnp.where(rope, jnp.cos(ang), 0.0))
    s_ref[...] = jnp.where(rope, jnp.sin(ang), 0.0)


def _rope_tables(pos_b, invf, tm):
    t = pos_b.shape[0]
    spec = pl.BlockSpec((tm, LANES), lambda i: (i, 0))
    return pl.pallas_call(
        _rope_body,
        grid=(t // tm,),
        in_specs=[spec, pl.BlockSpec((1, LANES), lambda i: (0, 0))],
        out_specs=[spec, spec],
        out_shape=[jax.ShapeDtypeStruct((t, LANES), F32)] * 2,
        compiler_params=_params(("parallel",)),
        name="rope_tables",
    )(pos_b, invf)


def _memkv_body(mem_ref, w_ref, o_ref):
    o_ref[0] = _dot(mem_ref[...], w_ref[0]).astype(BF16)


def _memkv(mem_b, w_b):
    depth = w_b.shape[0]
    rows = mem_b.shape[0]
    return pl.pallas_call(
        _memkv_body,
        grid=(depth,),
        in_specs=[pl.BlockSpec((rows, D_MODEL), lambda l: (0, 0)),
                  pl.BlockSpec((1, D_MODEL, 2 * X_WIDTH), lambda l: (l, 0, 0))],
        out_specs=pl.BlockSpec((1, rows, 2 * X_WIDTH), lambda l: (l, 0, 0)),
        out_shape=jax.ShapeDtypeStruct((depth, rows, 2 * X_WIDTH), BF16),
        compiler_params=_params(("parallel",)),
        name="mem_kv",
    )(mem_b, w_b)


_C_U = 0
_C_ZS = _C_U + SSM_WIDTH
_C_CQ = _C_ZS + SSM_WIDTH
_C_CKV = _C_CQ + MLA_Q_RANK
_C_KR = _C_CKV + MLA_KV_RANK
_C_ZM = _C_KR + 2 * LANES
_C_QX = _C_ZM + MLA_WIDTH
_C_ZX = _C_QX + X_WIDTH
_C_END = _C_ZX + X_WIDTH
_QW = MLA_HEADS * LANES


def _rms(v, g):
    return v * lax.rsqrt(jnp.mean(jnp.square(v), axis=-1, keepdims=True) + NORM_EPS) * g


def _inproj_body(x_ref, w1_ref, wq_ref, wk_ref, wvt_ref, qn_ref, kvn_ref, c_ref, s_ref, kvm_ref,
                 u_ref, vt_ref, szs_ref, q_ref, k_ref, szm_ref, ymem_ref, u_scr):
    xb = x_ref[...].astype(BF16)

    def proj(c0, c1):
        return _dot(xb, w1_ref[0, :, c0:c1])

    u = proj(_C_U, _C_ZS)
    n_c = u_scr.shape[1] // SSM_CHUNK
    gpt = LANES // SSM_GROUP
    for j in range(SSM_WIDTH // LANES):
        u_scr[j] = u[:, j * LANES:(j + 1) * LANES]
    for s in range(SSM_CHUNK):
        for j in range(SSM_WIDTH // LANES):
            a = u_scr[j, pl.ds(s, n_c, stride=SSM_CHUNK), :].astype(BF16)
            for g in range(gpt):
                u_ref[j * gpt + g, :, s * SSM_GROUP:(s + 1) * SSM_GROUP] = a[:, g * SSM_GROUP:(g + 1) * SSM_GROUP]
    szs_ref[...] = _silu(proj(_C_ZS, _C_CQ)).astype(BF16)
    szm_ref[...] = _silu(proj(_C_ZM, _C_QX)).astype(BF16)

    qx = proj(_C_QX, _C_ZX).astype(BF16)
    szx = _silu(proj(_C_ZX, _C_END))
    for h in range(X_HEADS):
        sl = slice(h * X_HEAD_DIM, (h + 1) * X_HEAD_DIM)
        km = kvm_ref[0, 0, :, sl]
        vm = kvm_ref[0, 0, :, X_WIDTH + h * X_HEAD_DIM:X_WIDTH + (h + 1) * X_HEAD_DIM]
        sc = _dot_nt(qx[:, sl], km) * X_HEAD_DIM ** -0.5
        pm = jnp.exp(sc - jnp.max(sc, axis=1, keepdims=True))
        om = _dot(pm.astype(BF16), vm) / jnp.sum(pm, axis=1, keepdims=True)
        ymem_ref[:, sl] = (om * szx[:, sl]).astype(BF16)

    cos = c_ref[...]
    sin = s_ref[...]
    scale = (MLA_NOPE + MLA_ROPE) ** -0.5 * math.log2(math.e)
    cos_q = cos * scale
    sin_q = sin * scale

    cq = _rms(proj(_C_CQ, _C_CKV), qn_ref[0]).astype(BF16)
    qa = _dot(cq, wq_ref[0])
    for h in range(MLA_HEADS):
        a = qa[:, h * LANES:(h + 1) * LANES]
        b = qa[:, _QW + h * LANES:_QW + (h + 1) * LANES]
        q_ref[:, h * LANES:(h + 1) * LANES] = (a * cos_q + b * sin_q).astype(BF16)

    ckv = _rms(proj(_C_CKV, _C_KR), kvn_ref[0]).astype(BF16)
    kn = _dot(ckv, wk_ref[0])
    kr = proj(_C_KR, _C_ZM)
    kr = kr[:, :LANES] * cos + kr[:, LANES:] * sin
    for h in range(MLA_HEADS):
        k_ref[:, h * LANES:(h + 1) * LANES] = (kn[:, h * LANES:(h + 1) * LANES] + kr).astype(BF16)
    sub = lax.broadcasted_iota(jnp.int32, (_QW, 1), 0)
    ones_row = jnp.where(sub % LANES == MLA_V, 1.0, 0.0)
    vt_ref[0, 0] = (_dot_nt(wvt_ref[0], ckv) + ones_row).astype(BF16)


def _inproj(x, w1, wq, wk, wvt, qn, kvn, cos, sin, kvm, layer, tm, seq):
    t = x.shape[0]
    n_c = tm // SSM_CHUNK
    nblk = seq // tm
    row = lambda w: pl.BlockSpec((tm, w), lambda i: (i, 0))
    lsel = lambda a: pl.BlockSpec((1,) + a.shape[1:], lambda i: (layer,) + (0,) * (a.ndim - 1))
    widths = (SSM_WIDTH, _QW, _QW, MLA_WIDTH, X_WIDTH)
    slab = (SSM_GROUPS, n_c, SSM_LC)
    return pl.pallas_call(
        _inproj_body,
        grid=(t // tm,),
        in_specs=[row(D_MODEL), lsel(w1), lsel(wq), lsel(wk), lsel(wvt), lsel(qn), lsel(kvn),
                  row(LANES), row(LANES),
                  pl.BlockSpec((1, 1) + kvm.shape[2:], lambda i: (layer, i // nblk, 0, 0))],
        out_specs=[pl.BlockSpec(slab, lambda i: (0, i, 0)),
                   pl.BlockSpec((1, 1, _QW, tm), lambda i: (i // nblk, i % nblk, 0, 0))]
        + [row(w) for w in widths],
        out_shape=[jax.ShapeDtypeStruct((SSM_GROUPS, t // SSM_CHUNK, SSM_LC), BF16),
                   jax.ShapeDtypeStruct((t // seq, nblk, _QW, tm), BF16)]
        + [jax.ShapeDtypeStruct((t, w), BF16) for w in widths],
        scratch_shapes=[pltpu.VMEM((SSM_WIDTH // LANES, tm, LANES), F32)],
        compiler_params=_params(("parallel",)),
        name="in_proj",
    )(x, w1, wq, wk, wvt, qn, kvn, cos, sin, kvm)


def _ssm_prep_body(*refs):
    for j in range(refs[0].shape[0]):
        _ssm_prep_one(*(r.at[pl.ds(j, 1)] for r in refs))


def _ssm_prep_one(ldt_ref, a2_ref, bt2_ref, ct2_ref, toep_ref, bs_ref, cs_ref, al_ref):
    chunk = SSM_CHUNK
    dt = jnp.exp(ldt_ref[0])
    lr2 = a2_ref[0, 0:1, :]
    li2 = a2_ref[0, 1:2, :]
    lane2 = lax.broadcasted_iota(jnp.int32, (1, 2 * SSM_STATE), 1)
    first = lane2 < SSM_STATE

    def power(e):
        m = jnp.exp(lr2 * dt * e)
        return m * jnp.cos(li2 * dt * e), m * jnp.sin(li2 * dt * e)

    def per_row(v, reps):
        return jnp.broadcast_to(v[:, None, :], (v.shape[0], reps, v.shape[1])).reshape(v.shape[0] * reps, v.shape[1])

    lb_re, lb_im = power(1.0)
    nr, ni = lb_re - 1.0, lb_im
    den = lr2 * lr2 + li2 * li2
    f_re = (nr * lr2 + ni * li2) / den
    f_im = (ni * lr2 - nr * li2) / den
    bt_re = bt2_ref[0, 0]
    bt_im = bt2_ref[0, 1]
    bb_re = f_re * bt_re - f_im * bt_im
    bb_im = f_re * bt_im + f_im * bt_re

    kcol = lax.broadcasted_iota(jnp.int32, (chunk, 1), 0).astype(F32)
    pk_re, pk_im = power(kcol)
    pe_re, pe_im = power(chunk - 1.0 - kcol)

    pr, pi = per_row(pe_re, SSM_GROUP), per_row(pe_im, SSM_GROUP)
    br = jnp.tile(bb_re, (chunk, 1))
    bi = jnp.tile(bb_im, (chunk, 1))
    s_re = pr * br - pi * bi
    s_im = pr * bi + pi * br
    bs_ref[0, :, :2 * SSM_STATE] = jnp.where(first, s_re, s_im).astype(BF16)
    bs_ref[0, :, 2 * SSM_STATE:] = jnp.where(first, s_im, s_re).astype(BF16)

    qr, qi = per_row(pk_re, SSM_GROUP), per_row(pk_im, SSM_GROUP)
    c_re = jnp.tile(ct2_ref[0, 0], (chunk, 1))
    c_im = jnp.tile(ct2_ref[0, 1], (chunk, 1))
    m_re = c_re * qr - c_im * qi
    m_im = c_re * qi + c_im * qr
    hi = lax.Precision.HIGHEST
    nt = (((1,), (1,)), ((), ()))
    kcat = (lax.dot_general(bb_re, jnp.where(first, m_re, 0.0), nt, precision=hi, preferred_element_type=F32)
            - lax.dot_general(bb_im, jnp.where(first, m_im, 0.0), nt, precision=hi, preferred_element_type=F32))
    lane = lax.broadcasted_iota(jnp.int32, (SSM_GROUP, SSM_LC), 1)
    for s in range(chunk):
        shifted = kcat if s == 0 else pltpu.roll(kcat, SSM_GROUP * s, 1)
        toep_ref[0, s * SSM_GROUP:(s + 1) * SSM_GROUP, :] = jnp.where(
            lane >= SSM_GROUP * s, shifted, 0.0).astype(BF16)

    n_re = m_re * lb_re - m_im * lb_im
    n_im = m_re * lb_im + m_im * lb_re
    cs_ref[0] = jnp.where(first, n_re, -n_im).T.astype(BF16)

    a_re, a_im = power(float(chunk))
    al_ref[0, 0:1, :] = a_re
    al_ref[0, 1:2, :] = jnp.where(first, -a_im, a_im)


def _ssm_prep(ldt, a2, bt2, ct2):
    n = ldt.shape[0]
    gp = SSM_GB
    sel = lambda a: pl.BlockSpec((gp,) + a.shape[1:], lambda i: (i,) + (0,) * (a.ndim - 1))
    out_shapes = [jax.ShapeDtypeStruct((n, SSM_LC, SSM_LC), BF16),
                  jax.ShapeDtypeStruct((n, SSM_LC, 4 * SSM_STATE), BF16),
                  jax.ShapeDtypeStruct((n, 2 * SSM_STATE, SSM_LC), BF16),
                  jax.ShapeDtypeStruct((n, 2, 2 * SSM_STATE), F32)]
    return pl.pallas_call(
        _ssm_prep_body,
        grid=(n // gp,),
        in_specs=[sel(ldt), sel(a2), sel(bt2), sel(ct2)],
        out_specs=[sel(s) for s in out_shapes],
        out_shape=out_shapes,
        compiler_params=_params(("parallel",)),
        name="ssm_prep",
    )(ldt, a2, bt2, ct2)


def _ssm_body(u_ref, toep_ref, bs_ref, cs_ref, al_ref, d_ref, o_ref, hloc_ref, hst_ref, *, batch, n_chunks):
    gb = u_ref.shape[0]
    w = 2 * SSM_STATE
    for j in range(gb):
        hl = _dot(u_ref[j], bs_ref[j])
        hloc_ref[j, 0] = hl[:, :w]
        hloc_ref[j, 1] = hl[:, w:]

    a1 = [jnp.broadcast_to(al_ref[j, 0:1, :], (batch, 2 * SSM_STATE)) for j in range(gb)]
    a2 = [jnp.broadcast_to(al_ref[j, 1:2, :], (batch, 2 * SSM_STATE)) for j in range(gb)]

    def step(c, carry):
        out = []
        rows = pl.ds(c, batch, stride=n_chunks)
        for j in range(gb):
            h, hs = carry[j]
            hst_ref[j, rows, :] = h
            out.append((a1[j] * h + a2[j] * hs + hloc_ref[j, 0, rows, :],
                        a1[j] * hs - a2[j] * h + hloc_ref[j, 1, rows, :]))
        return tuple(out)

    zero = jnp.zeros((batch, 2 * SSM_STATE), F32)
    lax.fori_loop(0, n_chunks, step, tuple((zero, zero) for _ in range(gb)), unroll=4)

    for j in range(gb):
        u = u_ref[j]
        y = (_dot(u, toep_ref[j]) + _dot(hst_ref[j].astype(BF16), cs_ref[j])
             + d_ref[j] * u.astype(F32))
        o_ref[j] = jax.nn.gelu(y).astype(BF16)


def _ssm(u_g, toep, bs, cs, al, d_e, layer, batch):
    groups, rows, _ = u_g.shape
    n_chunks = rows // batch
    gb = SSM_GB
    lsel = lambda a: pl.BlockSpec((gb,) + a.shape[1:],
                                  lambda i: (layer * (groups // gb) + i,) + (0,) * (a.ndim - 1))
    useg = pl.BlockSpec((gb, rows, SSM_LC), lambda i: (i, 0, 0))
    return pl.pallas_call(
        functools.partial(_ssm_body, batch=batch, n_chunks=n_chunks),
        grid=(groups // gb,),
        in_specs=[useg, lsel(toep), lsel(bs), lsel(cs), lsel(al), lsel(d_e)],
        out_specs=useg,
        out_shape=jax.ShapeDtypeStruct(u_g.shape, BF16),
        scratch_shapes=[pltpu.VMEM((gb, 2, rows, 2 * SSM_STATE), F32),
                        pltpu.VMEM((gb, rows, 2 * SSM_STATE), F32)],
        compiler_params=_params(("parallel",)),
        name="s5_scan",
    )(u_g, toep, bs, cs, al, d_e)


def _mla_body(q_ref, k_ref, vt_ref, szm_ref, o_ref, sa_ref, sb_ref, m_ref, acc_ref, *, tq):
    qi = pl.program_id(2)
    key = lax.broadcasted_iota(jnp.int32, (tq, tq), 0)
    qry = lax.broadcasted_iota(jnp.int32, (tq, tq), 1)
    causal = qry >= key
    heads = [slice(h * LANES, (h + 1) * LANES) for h in range(MLA_HB)]

    def scores(ki, dst_ref):
        k0 = pl.multiple_of(ki * tq, tq)
        for h, hl in enumerate(heads):
            dst_ref[h] = _dot_nt(k_ref[0, pl.ds(k0, tq), hl], q_ref[0, :, hl])

    def consume(ki, src_ref, masked):
        for h, hl in enumerate(heads):
            s = src_ref[h]
            if masked:
                s = jnp.where(causal, s, -1e30)
            m = m_ref[h]
            m_new = jnp.maximum(m, jnp.max(s, axis=0, keepdims=True))
            alpha = jnp.exp2(m - m_new)
            p = jnp.exp2((s - m_new).astype(BF16))
            acc_ref[h] = alpha * acc_ref[h] + _dot(vt_ref[0, ki, hl, :], p)
            m_ref[h] = m_new

    m_ref[...] = jnp.full(m_ref.shape, -1e30, F32)
    acc_ref[...] = jnp.zeros(acc_ref.shape, F32)
    scores(0, sa_ref)

    def pair(j, carry):
        scores(2 * j + 1, sb_ref)
        consume(2 * j, sa_ref, False)
        scores(2 * j + 2, sa_ref)
        consume(2 * j + 1, sb_ref, False)
        return carry

    lax.fori_loop(0, qi // 2, pair, 0)

    @pl.when(qi % 2 == 0)
    def _():
        consume(qi, sa_ref, True)

    @pl.when(qi % 2 == 1)
    def _():
        scores(qi, sb_ref)
        consume(qi - 1, sa_ref, False)
        consume(qi, sb_ref, True)

    lane = lax.broadcasted_iota(jnp.int32, (tq, LANES), 1)
    for hp in range(MLA_HB // 2):
        a0 = acc_ref[2 * hp]
        a1 = acc_ref[2 * hp + 1]
        o0 = (a0 / a0[MLA_V:MLA_V + 1, :]).T
        o1 = (a1 / a1[MLA_V:MLA_V + 1, :]).T
        o = jnp.where(lane < MLA_V, o0, pltpu.roll(o1, MLA_V, 1))
        sl = slice(hp * LANES, (hp + 1) * LANES)
        o_ref[0, :, sl] = (o * szm_ref[0, :, sl].astype(F32)).astype(BF16)


def _mla(q, k, vt, szm, tq):
    b, s, _ = q.shape
    hb = MLA_HB
    qk = lambda rows, blk: pl.BlockSpec((1, rows, hb * LANES), blk)
    outw = hb * MLA_V
    return pl.pallas_call(
        functools.partial(_mla_body, tq=tq),
        grid=(b, MLA_HEADS // hb, s // tq),
        in_specs=[qk(tq, lambda bi, hp, qi: (bi, qi, hp)),
                  qk(s, lambda bi, hp, qi: (bi, 0, hp)),
                  pl.BlockSpec((1, s // tq, hb * LANES, tq), lambda bi, hp, qi: (bi, 0, hp, 0)),
                  pl.BlockSpec((1, tq, outw), lambda bi, hp, qi: (bi, qi, hp))],
        out_specs=pl.BlockSpec((1, tq, outw), lambda bi, hp, qi: (bi, qi, hp)),
        out_shape=jax.ShapeDtypeStruct((b, s, MLA_WIDTH), BF16),
        scratch_shapes=[pltpu.VMEM((hb, tq, tq), F32), pltpu.VMEM((hb, tq, tq), F32),
                        pltpu.VMEM((hb, 1, tq), F32), pltpu.VMEM((hb, LANES, tq), F32)],
        compiler_params=_params(("parallel", "parallel", "arbitrary")),
        name="mla_attention",
    )(q, k, vt, szm)


def _merge_body(x_ref, g0_ref, gn_ref, szs_ref, ymla_ref, ymem_ref, wg_ref, bg_ref, wglu_ref, bglu_ref,
                pssm_ref, pmla_ref, pmem_ref, wout_ref, lng_ref, lnb_ref, o_ref, gt_scr, *, alpha):
    tiles = SSM_WIDTH // LANES

    def to_token_major(g_ref):
        n_c = g_ref.shape[1]
        gpt = LANES // SSM_GROUP
        lane = lax.broadcasted_iota(jnp.int32, (n_c, LANES), 1)
        piece = [(lane >= i * SSM_GROUP) & (lane < (i + 1) * SSM_GROUP) for i in range(gpt)]
        for j in range(tiles):
            for st in range(SSM_CHUNK // gpt):
                src = [g_ref[j * gpt + i, :, st * LANES:(st + 1) * LANES].astype(F32) for i in range(gpt)]
                for so in range(gpt):
                    merged = None
                    for i in range(gpt):
                        shift = ((i - so) * SSM_GROUP) % LANES
                        r = src[i] if shift == 0 else pltpu.roll(src[i], shift, 1)
                        merged = r if merged is None else jnp.where(piece[i], r, merged)
                    gt_scr[j, pl.ds(st * gpt + so, n_c, stride=SSM_CHUNK), :] = merged

    @pl.when(pl.program_id(0) == 0)
    def _():
        to_token_major(g0_ref)

    x = x_ref[...]
    xb = x.astype(BF16)
    gt = jnp.concatenate([gt_scr[j] for j in range(tiles)], axis=1).astype(BF16)
    to_token_major(gn_ref)
    t = _dot(gt, wglu_ref[0]) + bglu_ref[0]
    y_ssm = (t[:, :SSM_WIDTH] * _sigmoid(t[:, SSM_WIDTH:]) * szs_ref[...].astype(F32)).astype(BF16)
    merged = None
    for i, (y, p_ref) in enumerate(((y_ssm, pssm_ref), (ymla_ref[...], pmla_ref), (ymem_ref[...], pmem_ref))):
        cols = slice(i * D_MODEL, (i + 1) * D_MODEL)
        gate = _sigmoid(_dot(xb, wg_ref[0, :, cols]) + bg_ref[0, :, cols])
        term = gate * _dot(y, p_ref[0])
        merged = term if merged is None else merged + term
    r = alpha * x + _dot(merged.astype(BF16), wout_ref[0])
    mu = jnp.mean(r, axis=-1, keepdims=True)
    var = jnp.mean(jnp.square(r - mu), axis=-1, keepdims=True)
    o_ref[...] = (r - mu) * lax.rsqrt(var + NORM_EPS) * lng_ref[0] + lnb_ref[0]


def _merge(x, g, szs, ymla, ymem, wg, bg, wglu, bglu, pssm, pmla, pmem, wout, lng, lnb, layer, tm, alpha):
    t = x.shape[0]
    row = lambda w: pl.BlockSpec((tm, w), lambda i: (i, 0))
    lsel = lambda a: pl.BlockSpec((1,) + a.shape[1:], lambda i: (layer,) + (0,) * (a.ndim - 1))
    weights = (wg, bg, wglu, bglu, pssm, pmla, pmem, wout, lng, lnb)
    n_c = tm // SSM_CHUNK
    last = t // tm - 1
    slab = (SSM_GROUPS, n_c, SSM_LC)
    return pl.pallas_call(
        functools.partial(_merge_body, alpha=alpha),
        grid=(t // tm,),
        in_specs=[row(D_MODEL),
                  pl.BlockSpec(slab, lambda i: (0, 0, 0)),
                  pl.BlockSpec(slab, lambda i: (0, jnp.minimum(i + 1, last), 0)),
                  row(SSM_WIDTH), row(MLA_WIDTH), row(X_WIDTH)]
        + [lsel(w) for w in weights],
        out_specs=row(D_MODEL),
        out_shape=jax.ShapeDtypeStruct((t, D_MODEL), F32),
        scratch_shapes=[pltpu.VMEM((SSM_WIDTH // LANES, tm, LANES), F32)],
        compiler_params=_params(("arbitrary",)),
        name="merge_out_ln",
    )(x, g, g, szs, ymla, ymem, *weights)


def _rotate_half_cols(w):
    half = w.shape[-1] // 2
    return jnp.concatenate([-w[..., half:], w[..., :half]], axis=-1)


def _pack_weights(w_in, w_uq, w_ukv):
    depth = w_in.shape[0]
    w_in, w_uq, w_ukv = (a.astype(BF16) for a in (w_in, w_uq, w_ukv))
    kr0 = SSM_WIDTH + SSM_WIDTH + MLA_Q_RANK + MLA_KV_RANK
    kr1 = kr0 + MLA_ROPE
    end = kr1 + MLA_WIDTH + X_WIDTH + X_WIDTH
    w_kr = w_in[:, :, kr0:kr1]
    w_gate = w_in[:, :, end:]

    def rope_tile(w):
        z = jnp.zeros(w.shape[:-1] + (MLA_NOPE,), w.dtype)
        z2 = jnp.zeros(w.shape[:-1] + (LANES - MLA_NOPE - MLA_ROPE,), w.dtype)
        return jnp.concatenate([z, w, z2], axis=-1)

    w1 = jnp.concatenate([w_in[:, :, :kr0], rope_tile(w_kr), rope_tile(_rotate_half_cols(w_kr)),
                          w_in[:, :, kr1:end]], axis=-1)

    uq = w_uq.reshape(depth, MLA_Q_RANK, MLA_HEADS, MLA_NOPE + MLA_ROPE)
    q_nope, q_rope = uq[..., :MLA_NOPE], uq[..., MLA_NOPE:]
    pad = jnp.zeros(q_rope.shape, uq.dtype)
    q_main = jnp.concatenate([q_nope, q_rope, pad], axis=-1)
    q_rot = jnp.concatenate([jnp.zeros(q_nope.shape, uq.dtype), _rotate_half_cols(q_rope), pad], axis=-1)
    wq = jnp.concatenate([q_main.reshape(depth, MLA_Q_RANK, _QW), q_rot.reshape(depth, MLA_Q_RANK, _QW)],
                         axis=-1).astype(BF16)

    ukv = w_ukv.reshape(depth, MLA_KV_RANK, MLA_HEADS, MLA_NOPE + MLA_V)
    k_nope, v = ukv[..., :MLA_NOPE], ukv[..., MLA_NOPE:]
    k_tile = jnp.concatenate([k_nope, jnp.zeros(k_nope.shape, ukv.dtype)], axis=-1)
    wk = k_tile.reshape(depth, MLA_KV_RANK, _QW).astype(BF16)
    v_t = jnp.transpose(v, (0, 2, 3, 1))
    wvt = jnp.concatenate([v_t, jnp.zeros(v_t.shape, ukv.dtype)], axis=2).reshape(depth, _QW, MLA_KV_RANK)
    return w1, w_gate.astype(BF16), wq, wk, wvt.astype(BF16)


def _pack_ssm_params(a_re, a_im, log_dt, b_re, b_im, c_re, c_im, d):
    depth = a_re.shape[0]
    n = depth * SSM_GROUPS
    ar = a_re.reshape(n, SSM_STATE)
    ai = a_im.reshape(n, SSM_STATE)
    ldt = jnp.broadcast_to(log_dt.reshape(n, 1, 1), (n, 1, 2 * SSM_STATE))
    a2 = jnp.stack([jnp.concatenate([ar, ar], -1), jnp.concatenate([ai, ai], -1)], axis=1)
    bt = jnp.stack([b_re.reshape(n, SSM_STATE, SSM_GROUP), b_im.reshape(n, SSM_STATE, SSM_GROUP)], axis=1)
    bt = jnp.swapaxes(bt, -1, -2)
    bt2 = jnp.concatenate([bt, bt], axis=-1)
    ct = jnp.stack([c_re.reshape(n, SSM_GROUP, SSM_STATE), c_im.reshape(n, SSM_GROUP, SSM_STATE)], axis=1)
    ct2 = jnp.concatenate([ct, ct], axis=-1)
    d_e = jnp.tile(d.reshape(n, 1, SSM_GROUP), (1, 1, SSM_CHUNK))
    return ldt, a2, bt2, ct2, d_e


def kernel(x, mem, positions, w_in, b_gate, ssm_a_re, ssm_a_im, ssm_log_dt, ssm_b_re, ssm_b_im, ssm_c_re,
           ssm_c_im, ssm_d, w_glu, b_glu, mla_q_norm, w_uq, mla_kv_norm, w_ukv, w_mem_kv, p_ssm, p_mla, p_mem,
           w_out, ln_g, ln_b):
    bsz, s, d = x.shape
    depth = w_in.shape[0]
    mlen = mem.shape[1]
    t = bsz * s
    assert d == D_MODEL and s % SSM_CHUNK == 0 and SSM_GROUPS % SSM_GB == 0
    tm = min(512, s)
    tq = min(512, s)
    assert s % tm == 0 and s % tq == 0
    alpha = (2 * depth) ** 0.25
    n_chunks = s // SSM_CHUNK

    w1, w_gate, wq, wk, wvt = _pack_weights(w_in, w_uq, w_ukv)
    row3 = lambda a: a.reshape(depth, 1, a.shape[-1])
    qn, kvn = row3(mla_q_norm), row3(mla_kv_norm)
    bg, bglu, lng, lnb = row3(b_gate), row3(b_glu), row3(ln_g), row3(ln_b)
    wglu, pssm, pmla, pmem, wout = (a.astype(BF16) for a in (w_glu, p_ssm, p_mla, p_mem, w_out))

    inv_freq = ROPE_THETA ** (-jnp.arange(0, MLA_ROPE, 2, dtype=F32) / MLA_ROPE)
    invf = jnp.concatenate([jnp.zeros((MLA_NOPE,), F32), inv_freq, inv_freq,
                            jnp.zeros((LANES - MLA_NOPE - MLA_ROPE,), F32)]).reshape(1, LANES)
    pos_b = jnp.broadcast_to(positions.astype(F32).reshape(t, 1), (t, LANES))
    cos, sin = _rope_tables(pos_b, invf, tm)

    kvm = _memkv(mem.reshape(bsz * mlen, d).astype(BF16), w_mem_kv.astype(BF16))
    kvm = kvm.reshape(depth, bsz, mlen, 2 * X_WIDTH)

    ldt, a2, bt2, ct2, d_e = _pack_ssm_params(ssm_a_re, ssm_a_im, ssm_log_dt, ssm_b_re, ssm_b_im,
                                              ssm_c_re, ssm_c_im, ssm_d)
    toep, bs, cs, al = _ssm_prep(ldt, a2, bt2, ct2)

    xf = x.reshape(t, d)
    for layer in range(depth):
        u_g, vt, szs, q, k, szm, y_mem = _inproj(xf, w1, wq, wk, wvt, qn, kvn, cos, sin, kvm, layer, tm, s)
        g = _ssm(u_g, toep, bs, cs, al, d_e, layer, bsz)
        y_mla = _mla(q.reshape(bsz, s, _QW), k.reshape(bsz, s, _QW), vt, szm.reshape(bsz, s, MLA_WIDTH), tq)
        xf = _merge(xf, g, szs, y_mla.reshape(t, MLA_WIDTH), y_mem, w_gate, bg, wglu, bglu,
                    pssm, pmla, pmem, wout, lng, lnb, layer, tm, alpha)
    return xf.reshape(bsz, s, d)
```
